```python
import jax, jax.numpy as jnp
from jax import lax
import numpy as np

D_MODEL = 1024
BATCH = 8
SEQ = 8192
DEPTH = 2
DEC_BATCH = 32
DEC_SEQ = 16
PAST_LEN = 2048

CHUNK = 64
D_FF = 2816
EPS = 1e-6
F_MIN = 1e-30
POOL_WINDOWS = (2, 4, 8, 16)
POOL_GROUPS = 4
POOL_GROUP_DIM = 64
POOL_DIM = POOL_GROUPS * POOL_GROUP_DIM
POOL_HIST = max(POOL_WINDOWS) - 1
LRU_BLOCKS = 4
LRU_BLOCK_DIM = 64
LRU_DIM = LRU_BLOCKS * LRU_BLOCK_DIM
CONV_WIDTH = 4
LRU_C = 8.0
HG_HEADS = 4
HG_KDIM = 128
HG_VDIM = 128
HG_FDIM = HG_HEADS * HG_KDIM
HG_IDIM = HG_HEADS * HG_VDIM
MIX_DIM = POOL_DIM + LRU_DIM + HG_IDIM
IN_DIM = POOL_DIM + 2 * LRU_DIM + 2 * HG_FDIM + 2 * HG_IDIM
SPLITS = [POOL_DIM, POOL_DIM + LRU_DIM, POOL_DIM + 2 * LRU_DIM,
          POOL_DIM + 2 * LRU_DIM + HG_FDIM, POOL_DIM + 2 * LRU_DIM + 2 * HG_FDIM,
          POOL_DIM + 2 * LRU_DIM + 2 * HG_FDIM + HG_IDIM]

kernel_name = 'hybrid_pool_rglru_hgrn2_stream_step'


def rms_norm(x, g):
    xf = x.astype(jnp.float32)
    y = xf * lax.rsqrt(jnp.mean(xf * xf, axis=-1, keepdims=True) + EPS)
    return (y * g.astype(jnp.float32)).astype(x.dtype)


def swiglu(x, w_gate, w_up, w_down):
    return (jax.nn.silu(x @ w_gate) * (x @ w_up)) @ w_down


def pool_mixer(u, hist, pos0, w, scale):
    B, T, _ = u.shape
    up = jnp.concatenate([hist.astype(u.dtype), u], axis=1)
    cs = jnp.cumsum(up.astype(jnp.float32), axis=1)
    cs = jnp.concatenate([jnp.zeros((B, 1, POOL_DIM), jnp.float32), cs], axis=1)
    pos = pos0 + jnp.arange(T)
    end = cs[:, POOL_HIST + 1:]
    means = []
    for gi, win in enumerate(POOL_WINDOWS):
        sl = slice(gi * POOL_GROUP_DIM, (gi + 1) * POOL_GROUP_DIM)
        start = cs[:, POOL_HIST + 1 - win:POOL_HIST + 1 - win + T, sl]
        cnt = jnp.minimum(win, pos + 1).astype(jnp.float32)[None, :, None]
        means.append((end[..., sl] - start) / cnt)
    pooled = (jnp.concatenate(means, axis=-1) - u.astype(jnp.float32)).astype(u.dtype)
    pg = pooled.reshape(B, T, POOL_GROUPS, POOL_GROUP_DIM)
    y = jnp.einsum('btgc,gcd->btgd', pg, w).reshape(B, T, POOL_DIM) * scale
    return y, up[:, -POOL_HIST:]


def rglru_mixer(xb, gb, conv_hist, h0, pos0, conv_w, conv_b, w_a, b_a, w_x, b_x, lam):
    B, T, _ = xb.shape
    xp = jnp.concatenate([conv_hist.astype(xb.dtype), xb], axis=1)
    conv = jnp.broadcast_to(conv_b, (B, T, LRU_DIM)).astype(xb.dtype)
    for k in range(CONV_WIDTH):
        conv = conv + xp[:, k:k + T] * conv_w[k]
    xc = conv.reshape(B, T, LRU_BLOCKS, LRU_BLOCK_DIM)
    r = jax.nn.sigmoid(jnp.einsum('btgc,gcd->btgd', xc, w_a).reshape(B, T, LRU_DIM) + b_a)
    i = jax.nn.sigmoid(jnp.einsum('btgc,gcd->btgd', xc, w_x).reshape(B, T, LRU_DIM) + b_x)
    log_a = -LRU_C * jax.nn.softplus(-lam.astype(jnp.float32)) * r.astype(jnp.float32)
    a = jnp.exp(log_a)
    mult = jnp.sqrt(jnp.maximum(-jnp.expm1(2.0 * log_a), 0.0))
    pos = pos0 + jnp.arange(T)
    mult = jnp.where((pos == 0)[None, :, None], 1.0, mult)
    bterm = mult * (i * conv).astype(jnp.float32)
    bterm = bterm.at[:, 0].add(a[:, 0] * h0.astype(jnp.float32))

    def combine(left, right):
        a1, b1 = left
        a2, b2 = right
        return a1 * a2, a2 * b1 + b2

    _, h = lax.associative_scan(combine, (a, bterm), axis=1)
    y = (h * jax.nn.gelu(gb.astype(jnp.float32))).astype(xb.dtype)
    return y, xp[:, -(CONV_WIDTH - 1):], h[:, -1].astype(h0.dtype)


def hgrn2_mixer(q, fz, v, g, S0, lb, norm_g):
    B, T, _ = q.shape
    lb = lb.astype(jnp.float32)
    zf = fz.astype(jnp.float32)
    f = lb + (1.0 - lb) * jax.nn.sigmoid(zf)
    log_f = jnp.log(jnp.maximum(f, F_MIN))
    k = (1.0 - lb) * jax.nn.sigmoid(-zf)
    qf = jax.nn.silu(q.astype(jnp.float32))
    vf = v.astype(jnp.float32)
    n_chunks = -(-T // CHUNK)
    pad = n_chunks * CHUNK - T

    def to_chunks(t, dh):
        t = jnp.pad(t, ((0, 0), (0, pad), (0, 0)))
        return t.reshape(B, n_chunks, CHUNK, HG_HEADS, dh).transpose(1, 0, 3, 2, 4)

    qc, kc, fc = to_chunks(qf, HG_KDIM), to_chunks(k, HG_KDIM), to_chunks(log_f, HG_KDIM)
    vc = to_chunks(vf, HG_VDIM)
    causal = jnp.tril(jnp.ones((CHUNK, CHUNK), bool))[:, :, None]

    def step(S, inp):
        qb, kb, lfb, vb = inp
        b = jnp.cumsum(lfb, axis=-2)
        diff = b[..., :, None, :] - b[..., None, :, :]
        decay = jnp.where(causal, jnp.exp(jnp.where(causal, diff, 0.0)), 0.0)
        att = jnp.einsum('bhtk,bhsk,bhtsk->bhts', qb, kb, decay)
        o = jnp.einsum('bhts,bhsv->bhtv', att, vb) + jnp.einsum('bhtk,bhkv->bhtv', qb * jnp.exp(b), S)
        b_last = b[..., -1:, :]
        S_new = jnp.exp(b_last[..., 0, :])[..., None] * S + jnp.einsum(
            'bhsk,bhsv->bhkv', kb * jnp.exp(b_last - b), vb)
        return S_new, o

    S, o = lax.scan(step, S0.astype(jnp.float32), (qc, kc, fc, vc))
    o = o.transpose(1, 0, 3, 2, 4).reshape(B, n_chunks * CHUNK, HG_HEADS, HG_VDIM)[:, :T]
    o = o * lax.rsqrt(jnp.mean(o * o, axis=-1, keepdims=True) + EPS)
    o = o * norm_g.astype(jnp.float32).reshape(HG_HEADS, HG_VDIM)
    o = o.reshape(B, T, HG_IDIM) * jax.nn.silu(g.astype(jnp.float32))
    return o.astype(q.dtype), S.astype(S0.dtype)


def layer(x, pos0, pool_hist, conv_hist, h0, S0, lb, w):
    h = rms_norm(x, w['ffn1_norm'])
    x = x + 0.5 * swiglu(h, w['ffn1_w_gate'], w['ffn1_w_up'], w['ffn1_w_down'])
    h = rms_norm(x, w['mix_norm'])
    z = h @ w['w_in']
    u_a, xb, gb, q, fz, v, g = jnp.split(z, SPLITS, axis=-1)
    ya, new_pool = pool_mixer(u_a, pool_hist, pos0, w['pool_w'], w['pool_scale'])
    yb, new_conv, new_h = rglru_mixer(xb, gb, conv_hist, h0, pos0, w['conv_w'], w['conv_b'],
                                      w['lru_w_a'], w['lru_b_a'], w['lru_w_x'], w['lru_b_x'], w['lru_lambda'])
    yc, new_S = hgrn2_mixer(q, fz, v, g, S0, lb, w['hgrn_norm'])
    x = x + jnp.concatenate([ya, yb, yc], axis=-1) @ w['w_out']
    h = rms_norm(x, w['ffn2_norm'])
    x = x + 0.5 * swiglu(h, w['ffn2_w_gate'], w['ffn2_w_up'], w['ffn2_w_down'])
    return x, new_pool, new_conv, new_h, new_S


def setup_inputs(seed: int = 0) -> dict:
    key = jax.random.key(seed)
    ks = jax.random.split(key, 32)
    f32 = jnp.float32
    L = DEPTH

    def nrm(k, shape, scale):
        return scale * jax.random.normal(k, shape, f32)

    u = jax.random.uniform(ks[21], (L, LRU_DIM), f32, 0.9, 0.999)
    s = u ** (1.0 / LRU_C)
    lru_lambda = jnp.log(s) - jnp.log1p(-s)
    return {
        'x_prompt': nrm(ks[0], (BATCH, SEQ, D_MODEL), 1.0),
        'x_sample': nrm(ks[1], (DEC_BATCH, DEC_SEQ, D_MODEL), 1.0),
        'state_pool': nrm(ks[2], (L, DEC_BATCH, POOL_HIST, POOL_DIM), 1.0),
        'state_conv': nrm(ks[3], (L, DEC_BATCH, CONV_WIDTH - 1, LRU_DIM), 1.0),
        'state_lru': nrm(ks[4], (L, DEC_BATCH, LRU_DIM), 0.5),
        'state_hgrn': nrm(ks[5], (L, DEC_BATCH, HG_HEADS, HG_KDIM, HG_VDIM), 0.5),
        'ffn1_norm': 1.0 + nrm(ks[6], (L, D_MODEL), 0.02),
        'ffn1_w_gate': nrm(ks[7], (L, D_MODEL, D_FF), D_MODEL ** -0.5),
        'ffn1_w_up': nrm(ks[8], (L, D_MODEL, D_FF), D_MODEL ** -0.5),
        'ffn1_w_down': nrm(ks[9], (L, D_FF, D_MODEL), D_FF ** -0.5),
        'mix_norm': 1.0 + nrm(ks[10], (L, D_MODEL), 0.02),
        'w_in': nrm(ks[11], (L, D_MODEL, IN_DIM), D_MODEL ** -0.5),
        'pool_w': nrm(ks[12], (L, POOL_GROUPS, POOL_GROUP_DIM, POOL_GROUP_DIM), POOL_GROUP_DIM ** -0.5),
        'pool_scale': 1.0 + nrm(ks[13], (L, POOL_DIM), 0.02),
        'conv_w': nrm(ks[14], (L, CONV_WIDTH, LRU_DIM), CONV_WIDTH ** -0.5),
        'conv_b': nrm(ks[15], (L, LRU_DIM), 0.01),
        'lru_w_a': nrm(ks[16], (L, LRU_BLOCKS, LRU_BLOCK_DIM, LRU_BLOCK_DIM), LRU_BLOCK_DIM ** -0.5),
        'lru_b_a': nrm(ks[17], (L, LRU_DIM), 0.01),
        'lru_w_x': nrm(ks[18], (L, LRU_BLOCKS, LRU_BLOCK_DIM, LRU_BLOCK_DIM), LRU_BLOCK_DIM ** -0.5),
        'lru_b_x': nrm(ks[19], (L, LRU_DIM), 0.01),
        'lru_lambda': lru_lambda,
        'hgrn_lb_logits': nrm(ks[20], (L, HG_FDIM), 1.0),
        'hgrn_norm': 1.0 + nrm(ks[22], (L, HG_IDIM), 0.02),
        'w_out': nrm(ks[23], (L, MIX_DIM, D_MODEL), MIX_DIM ** -0.5),
        'ffn2_norm': 1.0 + nrm(ks[24], (L, D_MODEL), 0.02),
        'ffn2_w_gate': nrm(ks[25], (L, D_MODEL, D_FF), D_MODEL ** -0.5),
        'ffn2_w_up': nrm(ks[26], (L, D_MODEL, D_FF), D_MODEL ** -0.5),
        'ffn2_w_down': nrm(ks[27], (L, D_FF, D_MODEL), D_FF ** -0.5),
        'final_norm': 1.0 + nrm(ks[28], (D_MODEL,), 0.02),
    }


def reference(x_prompt, x_sample, state_pool, state_conv, state_lru, state_hgrn,
              ffn1_norm, ffn1_w_gate, ffn1_w_up, ffn1_w_down, mix_norm, w_in,
              pool_w, pool_scale, conv_w, conv_b, lru_w_a, lru_b_a, lru_w_x, lru_b_x, lru_lambda,
              hgrn_lb_logits, hgrn_norm, w_out, ffn2_norm, ffn2_w_gate, ffn2_w_up, ffn2_w_down,
              final_norm):
    lb_p = jax.nn.softmax(hgrn_lb_logits.astype(jnp.float32), axis=0)
    lower_bounds = jnp.maximum(jnp.cumsum(lb_p, axis=0) - lb_p[0], 0.0)

    def params(l):
        return {'ffn1_norm': ffn1_norm[l], 'ffn1_w_gate': ffn1_w_gate[l], 'ffn1_w_up': ffn1_w_up[l],
                'ffn1_w_down': ffn1_w_down[l], 'mix_norm': mix_norm[l], 'w_in': w_in[l],
                'pool_w': pool_w[l], 'pool_scale': pool_scale[l], 'conv_w': conv_w[l], 'conv_b': conv_b[l],
                'lru_w_a': lru_w_a[l], 'lru_b_a': lru_b_a[l], 'lru_w_x': lru_w_x[l], 'lru_b_x': lru_b_x[l],
                'lru_lambda': lru_lambda[l], 'hgrn_norm': hgrn_norm[l], 'w_out': w_out[l],
                'ffn2_norm': ffn2_norm[l], 'ffn2_w_gate': ffn2_w_gate[l], 'ffn2_w_up': ffn2_w_up[l],
                'ffn2_w_down': ffn2_w_down[l]}

    def run(x, pos0, pool, conv, lru, hg):
        pools, convs, lrus, hgs = [], [], [], []
        for l in range(DEPTH):
            x, sp, sc, sl, sh = layer(x, pos0, pool[l], conv[l], lru[l], hg[l], lower_bounds[l], params(l))
            pools.append(sp)
            convs.append(sc)
            lrus.append(sl)
            hgs.append(sh)
        return (rms_norm(x, final_norm), jnp.stack(pools), jnp.stack(convs),
                jnp.stack(lrus), jnp.stack(hgs))

    bp = x_prompt.shape[0]
    dt = x_prompt.dtype
    zero_pool = jnp.zeros((DEPTH, bp, POOL_HIST, POOL_DIM), dt)
    zero_conv = jnp.zeros((DEPTH, bp, CONV_WIDTH - 1, LRU_DIM), dt)
    zero_lru = jnp.zeros((DEPTH, bp, LRU_DIM), dt)
    zero_hgrn = jnp.zeros((DEPTH, bp, HG_HEADS, HG_KDIM, HG_VDIM), dt)
    y_prompt, pool_p, conv_p, lru_p, hgrn_p = run(x_prompt, 0, zero_pool, zero_conv, zero_lru, zero_hgrn)
    y_sample, pool_s, conv_s, lru_s, hgrn_s = run(x_sample, PAST_LEN, state_pool, state_conv, state_lru, state_hgrn)
    return (y_prompt, y_sample, pool_p, conv_p, lru_p, hgrn_p, pool_s, conv_s, lru_s, hgrn_s)
```

```python
import functools

import jax
import jax.numpy as jnp
from jax import lax
from jax.experimental import pallas as pl
from jax.experimental.pallas import tpu as pltpu

F32 = jnp.float32
BF16 = jnp.bfloat16

EPS = 1e-6
F_MIN = 1e-30
LRU_C = 8.0
POOL_WINDOWS = (2, 4, 8, 16)
POOL_GROUP_DIM = 64
POOL_DIM = 256
POOL_ROWS = 16
LRU_DIM = 256
CONV_WIDTH = 4
CONV_ROWS = 8
HG_HEADS = 4
HG_DIM = 128
HG_WIDTH = HG_HEADS * HG_DIM
HG_CHUNK = 64
SUBLANES = 8
O_POOL, O_LX, O_LG, O_Q, O_F, O_V, O_G, O_END = 0, 256, 512, 768, 1280, 1792, 2304, 2816
MIX_DIM = 1024

VMEM_LIMIT_BYTES = 56 * 1024 * 1024
FFN_ROWS = 512
MIX_ROWS = 512


def _rms(x, g):
    ms = jnp.mean(x * x, axis=-1, keepdims=True)
    return x * lax.rsqrt(ms + EPS) * g


def _dot(a, b):
    return jnp.dot(a, b, preferred_element_type=F32)


def _dot_nt(a, b):
    return lax.dot_general(a, b, (((1,), (1,)), ((), ())), preferred_element_type=F32)


def _dot_tn(a, b):
    return lax.dot_general(a, b, (((0,), (0,)), ((), ())), preferred_element_type=F32)


def _shift_rows(x, d):
    return pltpu.roll(x, d, axis=0)


def _ffn_kernel(*refs, final):
    if final:
        x_ref, g_ref, wg_ref, wu_ref, wd_ref, fg_ref, o_ref = refs
    else:
        x_ref, g_ref, wg_ref, wu_ref, wd_ref, o_ref = refs
    x = x_ref[...]
    h = _rms(x, g_ref[...]).astype(BF16)
    gate = _dot(h, wg_ref[...])
    up = _dot(h, wu_ref[...])
    act = (gate * jax.nn.sigmoid(gate) * up).astype(BF16)
    out = x + 0.5 * _dot(act, wd_ref[...])
    if final:
        out = _rms(out, fg_ref[...])
    o_ref[...] = out


def _ffn(x2d, norm_g, wg, wu, wd, final_g=None):
    n, d = x2d.shape
    dff = wg.shape[1]
    tm = min(FFN_ROWS, n)
    assert n % tm == 0
    final = final_g is not None
    const = lambda i: (0, 0)
    resident = functools.partial(pl.BlockSpec, index_map=const, pipeline_mode=pl.Buffered(1))
    in_specs = [pl.BlockSpec((tm, d), lambda i: (i, 0)),
                pl.BlockSpec((1, d), const),
                resident((d, dff)), resident((d, dff)), resident((dff, d))]
    args = [x2d, norm_g.reshape(1, d), wg, wu, wd]
    if final:
        in_specs.append(pl.BlockSpec((1, d), const))
        args.append(final_g.reshape(1, d))
    return pl.pallas_call(
        functools.partial(_ffn_kernel, final=final),
        grid=(n // tm,),
        in_specs=in_specs,
        out_specs=pl.BlockSpec((tm, d), lambda i: (i, 0)),
        out_shape=jax.ShapeDtypeStruct((n, d), F32),
        compiler_params=pltpu.CompilerParams(
            dimension_semantics=("parallel",), vmem_limit_bytes=VMEM_LIMIT_BYTES),
        name="ffn_final" if final else "ffn",
    )(*args)


def _pool_branch(u, hist, pos, wp, scale):
    t = u.shape[0]
    ext = jnp.concatenate([hist, u], axis=0)
    w2 = ext + _shift_rows(ext, 1)
    w4 = w2 + _shift_rows(w2, 2)
    w8 = w4 + _shift_rows(w4, 4)
    w16 = w8 + _shift_rows(w8, 8)
    lane = lax.broadcasted_iota(jnp.int32, (1, POOL_DIM), 1)
    group = lane // POOL_GROUP_DIM
    sums = jnp.where(group == 0, w2, jnp.where(group == 1, w4, jnp.where(group == 2, w8, w16)))
    sums = sums[POOL_ROWS:]
    win = jnp.where(group == 0, POOL_WINDOWS[0],
                    jnp.where(group == 1, POOL_WINDOWS[1],
                              jnp.where(group == 2, POOL_WINDOWS[2], POOL_WINDOWS[3])))
    cnt = jnp.minimum(win, pos + 1).astype(F32)
    pooled = sums / cnt - u
    ya = _dot(pooled.astype(BF16), wp) * scale
    return ya, ext[t:]


def _lru_branch(xb, gb, hist, h_prev, pos, conv_w, conv_b, wl, bl, lam):
    t = xb.shape[0]
    ext = jnp.concatenate([hist, xb], axis=0)
    conv = jnp.broadcast_to(conv_b, (t, LRU_DIM))
    for k in range(CONV_WIDTH):
        shifted = _shift_rows(ext, CONV_WIDTH - 1 - k) if k < CONV_WIDTH - 1 else ext
        conv = conv + shifted[CONV_ROWS:] * conv_w[k:k + 1]
    gates = _dot(conv.astype(BF16), wl) + bl
    r = jax.nn.sigmoid(gates[:, :LRU_DIM])
    ig = jax.nn.sigmoid(gates[:, LRU_DIM:])
    log_a = (-LRU_C * jax.nn.softplus(-lam)) * r
    a = jnp.exp(log_a)
    mult = jnp.sqrt(jnp.maximum(1.0 - a * a, 0.0))
    mult = jnp.where(pos == 0, 1.0, mult)
    bterm = mult * (ig * conv)
    row = lax.broadcasted_iota(jnp.int32, (t, 1), 0)
    acc_a, acc_b = a, bterm
    d = 1
    while d < t:
        keep = row >= d
        prev_a = jnp.where(keep, _shift_rows(acc_a, d), 1.0)
        prev_b = jnp.where(keep, _shift_rows(acc_b, d), 0.0)
        acc_b = acc_a * prev_b + acc_b
        acc_a = acc_a * prev_a
        d *= 2
    h = acc_a * h_prev + acc_b
    yb = h * jax.nn.gelu(gb)
    return yb, ext[t:], h[t - SUBLANES:]


def _hgrn_chunk(q, fz, v, g, lb, norm_g, st_ref):
    c = q.shape[0]
    one_m_lb = 1.0 - lb
    f = lb + one_m_lb * jax.nn.sigmoid(fz)
    log_f = jnp.log(jnp.maximum(f, F_MIN))
    kk = one_m_lb * jax.nn.sigmoid(-fz)
    qf = q * jax.nn.sigmoid(q)

    row = lax.broadcasted_iota(jnp.int32, (c, 1), 0)
    b = log_f
    d = 1
    while d < c:
        b = b + jnp.where(row >= d, _shift_rows(b, d), 0.0)
        d *= 2
    b_last = b[c - 1:c]
    q_in = (qf * jnp.exp(b)).astype(BF16)
    k_out = (kk * jnp.exp(b_last - b)).astype(BF16)
    decay = jnp.exp(b_last)
    v16 = v.astype(BF16)

    levels = []
    hs = c // 2
    while hs >= SUBLANES:
        nblk = c // (2 * hs)
        ref = jnp.concatenate(
            [jnp.broadcast_to(b[m * 2 * hs + hs - 1:m * 2 * hs + hs], (2 * hs, HG_WIDTH))
             for m in range(nblk)], axis=0) if nblk > 1 else b[hs - 1:hs]
        e = jnp.exp(-jnp.abs(b - ref))
        upper = (row & hs) != 0
        q_l = jnp.where(upper, qf * e, 0.0).astype(BF16)
        k_l = jnp.where(upper, 0.0, kk * e).astype(BF16)
        levels.append((hs, q_l, k_l))
        hs //= 2

    in_tile = row & (SUBLANES - 1)
    diag_w = [qf * kk]
    diag_v = [v]
    for j in range(1, min(SUBLANES, c)):
        valid = in_tile >= j
        w = jnp.exp(jnp.where(valid, b - _shift_rows(b, j), 0.0))
        diag_w.append(jnp.where(valid, qf * _shift_rows(kk, j) * w, 0.0))
        diag_v.append(_shift_rows(v, j))

    ti = lax.broadcasted_iota(jnp.int32, (c, c), 0)
    si = lax.broadcasted_iota(jnp.int32, (c, c), 1)
    blk_xor = ti ^ si

    outs = []
    for h in range(HG_HEADS):
        sl = slice(h * HG_DIM, (h + 1) * HG_DIM)
        st = st_ref[h]
        o = _dot_nt(q_in[:, sl], st.astype(BF16))
        if levels:
            att = None
            for hs, q_l, k_l in levels:
                a_l = _dot_nt(q_l[:, sl], k_l[:, sl])
                if 2 * hs < c:
                    a_l = jnp.where(blk_xor < 2 * hs, a_l, 0.0)
                att = a_l if att is None else att + a_l
            o = o + _dot(att.astype(BF16), v16[:, sl])
        for w, vs in zip(diag_w, diag_v):
            o = o + jnp.sum(w[:, sl], axis=-1, keepdims=True) * vs[:, sl]
        st_ref[h] = decay[:, sl] * st + _dot_tn(v16[:, sl], k_out[:, sl])
        o = o * lax.rsqrt(jnp.mean(o * o, axis=-1, keepdims=True) + EPS)
        outs.append(o)
    o = jnp.concatenate(outs, axis=-1) * norm_g
    return o * (g * jax.nn.sigmoid(g))


def _mixer_kernel(x_ref, ng_ref, win_ref, wp_ref, ps_ref, cw_ref, cb_ref, wl_ref, bl_ref, lam_ref,
                  lbl_ref, hn_ref, wout_ref, pool0_ref, conv0_ref, h0_ref, s0_ref,
                  y_ref, pool_o_ref, conv_o_ref, h_o_ref, s_o_ref,
                  z_scr, mix_scr, pool_c, conv_c, h_c, st_c, *, layer, pos0, chunk):
    i = pl.program_id(1)
    tt = x_ref.shape[0]

    @pl.when(i == 0)
    def _():
        pool_c[...] = pool0_ref[...]
        conv_c[...] = conv0_ref[...]
        h_c[...] = h0_ref[...]
        for h in range(HG_HEADS):
            st_c[h] = s0_ref[h].T

    x = x_ref[...]
    hn = _rms(x, ng_ref[...]).astype(BF16)
    z_scr[...] = _dot(hn, win_ref[...])
    pos = pos0 + i * tt + lax.broadcasted_iota(jnp.int32, (tt, 1), 0)

    ya, new_pool = _pool_branch(z_scr[:, O_POOL:O_LX], pool_c[...], pos, wp_ref[...], ps_ref[...])
    pool_c[...] = new_pool
    mix_scr[:, 0:POOL_DIM] = ya.astype(BF16)

    yb, new_conv, new_h = _lru_branch(
        z_scr[:, O_LX:O_LG], z_scr[:, O_LG:O_Q], conv_c[...], h_c[SUBLANES - 1:SUBLANES],
        pos, cw_ref[...], cb_ref[...], wl_ref[...], bl_ref[...], lam_ref[...])
    conv_c[...] = new_conv
    h_c[...] = new_h
    mix_scr[:, POOL_DIM:POOL_DIM + LRU_DIM] = yb.astype(BF16)

    logits = lbl_ref[...]
    ex = jnp.exp(logits - jnp.max(logits, axis=0, keepdims=True))
    p = ex / jnp.sum(ex, axis=0, keepdims=True)
    lb = jnp.maximum(jnp.sum(p[0:layer + 1], axis=0, keepdims=True) - p[0:1], 0.0)
    hnorm = hn_ref[...]

    def chunk_body(ci, carry):
        r0 = pl.multiple_of(ci * chunk, chunk)
        rows = pl.ds(r0, chunk)
        yc = _hgrn_chunk(z_scr[rows, O_Q:O_F], z_scr[rows, O_F:O_V], z_scr[rows, O_V:O_G],
                         z_scr[rows, O_G:O_END], lb, hnorm, st_c)
        mix_scr[rows, POOL_DIM + LRU_DIM:MIX_DIM] = yc.astype(BF16)
        return carry

    lax.fori_loop(0, tt // chunk, chunk_body, 0)

    y_ref[...] = x + _dot(mix_scr[...], wout_ref[...])

    @pl.when(i == pl.num_programs(1) - 1)
    def _():
        pool_o_ref[...] = pool_c[...]
        conv_o_ref[...] = conv_c[...]
        h_o_ref[...] = h_c[...]
        for h in range(HG_HEADS):
            s_o_ref[h] = st_c[h].T


def _block_diag(w):
    g, c, d = w.shape
    eye = jnp.eye(g, dtype=w.dtype)
    return (eye[:, None, :, None] * w[:, :, None, :]).reshape(g * c, g * d)


def _mixer(x, pos0, pool0, conv0, h0, s0, layer, p):
    bsz, t, d = x.shape
    tt = min(MIX_ROWS, t)
    assert t % tt == 0 and tt % SUBLANES == 0 and tt >= POOL_ROWS
    chunk = min(HG_CHUNK, tt)
    assert tt % chunk == 0
    in_dim = p['w_in'].shape[1]
    depth = p['lb_logits'].shape[0]

    pool0 = jnp.pad(pool0, ((0, 0), (POOL_ROWS - pool0.shape[1], 0), (0, 0)))
    conv0 = jnp.pad(conv0, ((0, 0), (CONV_ROWS - conv0.shape[1], 0), (0, 0)))
    h0 = jnp.pad(h0[:, None, :], ((0, 0), (SUBLANES - 1, 0), (0, 0)))

    def const(shape):
        return pl.BlockSpec(shape, lambda b, i: (0,) * len(shape))

    def per_seq(shape):
        return pl.BlockSpec((None,) + shape, lambda b, i: (b,) + (0,) * len(shape))

    in_specs = [
        pl.BlockSpec((None, tt, d), lambda b, i: (b, i, 0)),
        const((1, d)), const((d, in_dim)),
        const((POOL_DIM, POOL_DIM)), const((1, POOL_DIM)),
        const((CONV_WIDTH, LRU_DIM)), const((1, LRU_DIM)),
        const((LRU_DIM, 2 * LRU_DIM)), const((1, 2 * LRU_DIM)), const((1, LRU_DIM)),
        const((depth, HG_WIDTH)), const((1, HG_WIDTH)),
        const((MIX_DIM, d)),
        per_seq((POOL_ROWS, POOL_DIM)), per_seq((CONV_ROWS, LRU_DIM)), per_seq((SUBLANES, LRU_DIM)),
        per_seq((HG_HEADS, HG_DIM, HG_DIM)),
    ]
    out_specs = [
        pl.BlockSpec((None, tt, d), lambda b, i: (b, i, 0)),
        per_seq((POOL_ROWS, POOL_DIM)), per_seq((CONV_ROWS, LRU_DIM)), per_seq((SUBLANES, LRU_DIM)),
        per_seq((HG_HEADS, HG_DIM, HG_DIM)),
    ]
    out_shape = [
        jax.ShapeDtypeStruct((bsz, t, d), F32),
        jax.ShapeDtypeStruct((bsz, POOL_ROWS, POOL_DIM), F32),
        jax.ShapeDtypeStruct((bsz, CONV_ROWS, LRU_DIM), F32),
        jax.ShapeDtypeStruct((bsz, SUBLANES, LRU_DIM), F32),
        jax.ShapeDtypeStruct((bsz, HG_HEADS, HG_DIM, HG_DIM), F32),
    ]
    scratch = [
        pltpu.VMEM((tt, in_dim), F32), pltpu.VMEM((tt, MIX_DIM), BF16),
        pltpu.VMEM((POOL_ROWS, POOL_DIM), F32), pltpu.VMEM((CONV_ROWS, LRU_DIM), F32),
        pltpu.VMEM((SUBLANES, LRU_DIM), F32), pltpu.VMEM((HG_HEADS, HG_DIM, HG_DIM), F32),
    ]
    y, pool_n, conv_n, h_n, s_n = pl.pallas_call(
        functools.partial(_mixer_kernel, layer=layer, pos0=pos0, chunk=chunk),
        grid=(bsz, t // tt),
        in_specs=in_specs, out_specs=out_specs, out_shape=out_shape, scratch_shapes=scratch,
        compiler_params=pltpu.CompilerParams(
            dimension_semantics=("parallel", "arbitrary"), vmem_limit_bytes=VMEM_LIMIT_BYTES),
        name="mixer",
    )(x, p['mix_norm'], p['w_in'], p['pool_wbd'], p['pool_scale'], p['conv_w'], p['conv_b'],
      p['lru_wbd'], p['lru_b'], p['lru_lambda'], p['lb_logits'], p['hgrn_norm'], p['w_out'],
      pool0, conv0, h0, s0)
    return (y, pool_n[:, POOL_ROWS - 15:], conv_n[:, CONV_ROWS - (CONV_WIDTH - 1):],
            h_n[:, SUBLANES - 1], s_n)


def _run(x, pos0, pool, conv, lru, hg, layers, final_norm):
    bsz, t, d = x.shape
    depth = len(layers)
    pools, convs, lrus, hgs = [], [], [], []
    for l, p in enumerate(layers):
        x = _ffn(x.reshape(bsz * t, d), p['ffn1_norm'], p['ffn1_w_gate'], p['ffn1_w_up'],
                 p['ffn1_w_down']).reshape(bsz, t, d)
        x, sp, sc, sl, sh = _mixer(x, pos0, pool[l], conv[l], lru[l], hg[l], l, p)
        x = _ffn(x.reshape(bsz * t, d), p['ffn2_norm'], p['ffn2_w_gate'], p['ffn2_w_up'],
                 p['ffn2_w_down'], final_norm if l == depth - 1 else None).reshape(bsz, t, d)
        pools.append(sp)
        convs.append(sc)
        lrus.append(sl)
        hgs.append(sh)
    return x, jnp.stack(pools), jnp.stack(convs), jnp.stack(lrus), jnp.stack(hgs)


def kernel(x_prompt, x_sample, state_pool, state_conv, state_lru, state_hgrn, ffn1_norm, ffn1_w_gate, ffn1_w_up, ffn1_w_down, mix_norm, w_in, pool_w, pool_scale, conv_w, conv_b, lru_w_a, lru_b_a, lru_w_x, lru_b_x, lru_lambda, hgrn_lb_logits, hgrn_norm, w_out, ffn2_norm, ffn2_w_gate, ffn2_w_up, ffn2_w_down, final_norm):
    depth = w_in.shape[0]
    past_len = 2048
    layers = []
    for l in range(depth):
        layers.append({
            'ffn1_norm': ffn1_norm[l], 'ffn1_w_gate': ffn1_w_gate[l].astype(BF16),
            'ffn1_w_up': ffn1_w_up[l].astype(BF16), 'ffn1_w_down': ffn1_w_down[l].astype(BF16),
            'mix_norm': mix_norm[l][None], 'w_in': w_in[l].astype(BF16),
            'pool_wbd': _block_diag(pool_w[l]).astype(BF16), 'pool_scale': pool_scale[l][None],
            'conv_w': conv_w[l], 'conv_b': conv_b[l][None],
            'lru_wbd': jnp.concatenate([_block_diag(lru_w_a[l]), _block_diag(lru_w_x[l])],
                                       axis=1).astype(BF16),
            'lru_b': jnp.concatenate([lru_b_a[l], lru_b_x[l]])[None],
            'lru_lambda': lru_lambda[l][None], 'lb_logits': hgrn_lb_logits,
            'hgrn_norm': hgrn_norm[l][None], 'w_out': w_out[l].astype(BF16),
            'ffn2_norm': ffn2_norm[l], 'ffn2_w_gate': ffn2_w_gate[l].astype(BF16),
            'ffn2_w_up': ffn2_w_up[l].astype(BF16), 'ffn2_w_down': ffn2_w_down[l].astype(BF16),
        })
    bp = x_prompt.shape[0]
    dt = x_prompt.dtype
    zero_pool = jnp.zeros((depth, bp) + state_pool.shape[2:], dt)
    zero_conv = jnp.zeros((depth, bp) + state_conv.shape[2:], dt)
    zero_lru = jnp.zeros((depth, bp) + state_lru.shape[2:], dt)
    zero_hgrn = jnp.zeros((depth, bp) + state_hgrn.shape[2:], dt)
    y_p, pool_p, conv_p, lru_p, hgrn_p = _run(x_prompt, 0, zero_pool, zero_conv, zero_lru, zero_hgrn,
                                              layers, final_norm)
    y_s, pool_s, conv_s, lru_s, hgrn_s = _run(x_sample, past_len, state_pool, state_conv, state_lru,
                                              state_hgrn, layers, final_norm)
    return (y_p, y_s, pool_p, conv_p, lru_p, hgrn_p, pool_s, conv_s, lru_s, hgrn_s)
```

```python
import functools

import jax
import jax.numpy as jnp
from jax import lax
from jax.experimental import pallas as pl
from jax.experimental.pallas import tpu as pltpu

F32 = jnp.float32
BF16 = jnp.bfloat16

EPS = 1e-6
F_MIN = 1e-30
LRU_C = 8.0
POOL_WINDOWS = (2, 4, 8, 16)
POOL_GROUP_DIM = 64
POOL_DIM = 256
POOL_HIST = 15
POOL_ROWS = 16
LRU_DIM = 256
CONV_WIDTH = 4
CONV_ROWS = 8
HG_HEADS = 4
HG_DIM = 128
HG_WIDTH = HG_HEADS * HG_DIM
HG_CHUNK = 256
HG_BLOCK = 256
HG_UNROLL = 2
SUBLANES = 8
O_POOL, O_LX, O_LG, O_Q, O_F, O_V, O_G, O_END = 0, 256, 512, 768, 1280, 1792, 2304, 2816
MIX_DIM = 1024
PAST_LEN = 2048

VMEM_LIMIT_BYTES = 56 * 1024 * 1024
FFN_ROWS = 512
MIX_ROWS = 512


def _rms(x, g):
    ms = jnp.mean(x * x, axis=-1, keepdims=True)
    return x * lax.rsqrt(ms + EPS) * g


def _dot(a, b):
    return jnp.dot(a, b, preferred_element_type=F32)


def _dot_nt(a, b):
    return lax.dot_general(a, b, (((1,), (1,)), ((), ())), preferred_element_type=F32)


def _dot_tn(a, b):
    return lax.dot_general(a, b, (((0,), (0,)), ((), ())), preferred_element_type=F32)


def _shift_rows(x, d):
    return pltpu.roll(x, d, axis=0)


def _tile_view(x):
    return x.reshape(x.shape[0] // SUBLANES, SUBLANES, x.shape[1])


def _row_of_tile(x3, j):
    return jnp.broadcast_to(x3[:, j:j + 1, :], x3.shape)


def _const_spec(shape, single=False):
    index_map = lambda *_: (0,) * len(shape)
    if single:
        return pl.BlockSpec(shape, index_map, pipeline_mode=pl.Buffered(1))
    return pl.BlockSpec(shape, index_map)


def _params(semantics):
    return pltpu.CompilerParams(dimension_semantics=semantics, vmem_limit_bytes=VMEM_LIMIT_BYTES)


def _ffn_kernel(*refs, final):
    if final:
        x_ref, g_ref, wg_ref, wu_ref, wd_ref, fg_ref, o_ref = refs
    else:
        x_ref, g_ref, wg_ref, wu_ref, wd_ref, o_ref = refs
    x = x_ref[...]
    h = _rms(x, g_ref[...]).astype(BF16)
    gate = _dot(h, wg_ref[...])
    up = _dot(h, wu_ref[...])
    act = (gate * jax.nn.sigmoid(gate) * up).astype(BF16)
    out = x + 0.5 * _dot(act, wd_ref[...])
    if final:
        out = _rms(out, fg_ref[...])
    o_ref[...] = out


def _ffn(x2d, norm_g, wg, wu, wd, final_g=None):
    n, d = x2d.shape
    dff = wg.shape[1]
    tm = min(FFN_ROWS, n)
    assert n % tm == 0
    final = final_g is not None
    rows = pl.BlockSpec((tm, d), lambda i: (i, 0))
    in_specs = [rows, _const_spec((1, d)),
                _const_spec((d, dff), True), _const_spec((d, dff), True), _const_spec((dff, d), True)]
    args = [x2d, norm_g, wg, wu, wd]
    if final:
        in_specs.append(_const_spec((1, d)))
        args.append(final_g)
    return pl.pallas_call(
        functools.partial(_ffn_kernel, final=final),
        grid=(n // tm,),
        in_specs=in_specs,
        out_specs=rows,
        out_shape=jax.ShapeDtypeStruct((n, d), F32),
        compiler_params=_params(("parallel",)),
        name="ffn_final" if final else "ffn",
    )(*args)


def _pool_branch(u, hist, pos, wp, scale):
    t = u.shape[0]
    ext = jnp.concatenate([hist, u], axis=0)
    w2 = ext + _shift_rows(ext, 1)
    w4 = w2 + _shift_rows(w2, 2)
    w8 = w4 + _shift_rows(w4, 4)
    w16 = w8 + _shift_rows(w8, 8)
    lane = lax.broadcasted_iota(jnp.int32, (1, POOL_DIM), 1)
    group = lane // POOL_GROUP_DIM
    sums = jnp.where(group == 0, w2, jnp.where(group == 1, w4, jnp.where(group == 2, w8, w16)))
    sums = sums[POOL_ROWS:]
    win = jnp.where(group == 0, POOL_WINDOWS[0],
                    jnp.where(group == 1, POOL_WINDOWS[1],
                              jnp.where(group == 2, POOL_WINDOWS[2], POOL_WINDOWS[3])))
    cnt = jnp.minimum(win, pos + 1).astype(F32)
    pooled = sums / cnt - u
    ya = _dot(pooled.astype(BF16), wp) * scale
    return ya, ext[t:]


def _lru_branch(xb, gb, hist, h_prev, pos, conv_w, conv_b, wl, bl, lam):
    t = xb.shape[0]
    ext = jnp.concatenate([hist, xb], axis=0)
    conv = jnp.broadcast_to(conv_b, (t, LRU_DIM))
    for k in range(CONV_WIDTH):
        shifted = _shift_rows(ext, CONV_WIDTH - 1 - k) if k < CONV_WIDTH - 1 else ext
        conv = conv + shifted[CONV_ROWS:] * conv_w[k:k + 1]
    gates = _dot(conv.astype(BF16), wl) + bl
    r = jax.nn.sigmoid(gates[:, :LRU_DIM])
    ig = jax.nn.sigmoid(gates[:, LRU_DIM:])
    log_a = (-LRU_C * jax.nn.softplus(-lam)) * r
    a = jnp.exp(log_a)
    mult = jnp.sqrt(jnp.maximum(1.0 - a * a, 0.0))
    mult = jnp.where(pos == 0, 1.0, mult)
    bterm = mult * (ig * conv)
    acc_a, acc_b = _tile_view(a), _tile_view(bterm)
    sub = lax.broadcasted_iota(jnp.int32, (1, SUBLANES, 1), 1)
    d = 1
    while d < SUBLANES:
        keep = sub >= d
        prev_a = jnp.where(keep, pltpu.roll(acc_a, d, axis=1), 1.0)
        prev_b = jnp.where(keep, pltpu.roll(acc_b, d, axis=1), 0.0)
        acc_b = acc_a * prev_b + acc_b
        acc_a = acc_a * prev_a
        d *= 2
    tiles = []
    carry = h_prev
    for i in range(t // SUBLANES):
        h_i = acc_a[i] * carry + acc_b[i]
        tiles.append(h_i)
        carry = h_i[SUBLANES - 1:SUBLANES]
    h = jnp.concatenate(tiles, axis=0)
    yb = h * jax.nn.gelu(gb)
    return yb, ext[t:], tiles[-1]


def _hgrn_chunk(q, fz, v, g, lb, norm_g, st_ref):
    c = q.shape[0]
    nt = c // SUBLANES
    one_m_lb = 1.0 - lb
    sig = jax.nn.sigmoid(fz)
    fc = jnp.maximum(lb + one_m_lb * sig, F_MIN)
    log_f = jnp.log(fc)
    kk = one_m_lb * (1.0 - sig)
    qf = q * jax.nn.sigmoid(q)

    sub = lax.broadcasted_iota(jnp.int32, (1, SUBLANES, 1), 1)
    c3 = _tile_view(log_f)
    d = 1
    while d < SUBLANES:
        c3 = c3 + jnp.where(sub >= d, pltpu.roll(c3, d, axis=1), 0.0)
        d *= 2
    carries = [jnp.zeros((1, 1, HG_WIDTH), F32)]
    for i in range(1, nt):
        carries.append(carries[-1] + c3[i - 1:i, SUBLANES - 1:SUBLANES, :])
    b = (c3 + jnp.concatenate(carries, axis=0)).reshape(c, HG_WIDTH)
    b_last = b[c - 1:c]
    q_in = (qf * jnp.exp(b)).astype(BF16)
    k_out = (kk * jnp.exp(b_last - b)).astype(BF16)
    decay = jnp.exp(b_last)
    v16 = v.astype(BF16)

    blk = min(HG_BLOCK, c)
    wide = []
    hs = c // 2
    while hs >= blk:
        for m in range(c // (2 * hs)):
            lo, mid, hi = m * 2 * hs, m * 2 * hs + hs, (m + 1) * 2 * hs
            ref = b[mid - 1:mid]
            wide.append((lo, mid, hi, (qf[mid:hi] * jnp.exp(b[mid:hi] - ref)).astype(BF16),
                         (kk[lo:mid] * jnp.exp(ref - b[lo:mid])).astype(BF16)))
        hs //= 2
    levels = []
    while hs >= SUBLANES:
        q_parts, k_parts = [], []
        zero = jnp.zeros((hs, HG_WIDTH), BF16)
        for m in range(c // (2 * hs)):
            lo, mid, hi = m * 2 * hs, m * 2 * hs + hs, (m + 1) * 2 * hs
            ref = b[mid - 1:mid]
            k_parts += [(kk[lo:mid] * jnp.exp(ref - b[lo:mid])).astype(BF16), zero]
            q_parts += [zero, (qf[mid:hi] * jnp.exp(b[mid:hi] - ref)).astype(BF16)]
        levels.append((hs, jnp.concatenate(q_parts, axis=0), jnp.concatenate(k_parts, axis=0)))
        hs //= 2
    qf3, kk3 = _tile_view(qf), _tile_view(kk)
    while hs >= 2:
        if 2 * hs == SUBLANES:
            ref = _row_of_tile(c3, hs - 1)
        else:
            ref = jnp.where(sub < 2 * hs, _row_of_tile(c3, hs - 1), _row_of_tile(c3, 3 * hs - 1))
        e = jnp.exp(-jnp.abs(c3 - ref))
        upper = (sub & hs) != 0
        q_l = jnp.where(upper, qf3 * e, 0.0).astype(BF16).reshape(c, HG_WIDTH)
        k_l = jnp.where(upper, 0.0, kk3 * e).astype(BF16).reshape(c, HG_WIDTH)
        levels.append((hs, q_l, k_l))
        hs //= 2
    odd = (sub & 1) != 0
    q_l = jnp.where(odd, qf3 * _tile_view(fc), 0.0).astype(BF16).reshape(c, HG_WIDTH)
    k_l = jnp.where(odd, 0.0, kk3).astype(BF16).reshape(c, HG_WIDTH)
    levels.append((1, q_l, k_l))
    qk = qf * kk

    ti = lax.broadcasted_iota(jnp.int32, (blk, blk), 0)
    si = lax.broadcasted_iota(jnp.int32, (blk, blk), 1)
    blk_xor = ti ^ si

    outs = []
    for h in range(HG_HEADS):
        sl = slice(h * HG_DIM, (h + 1) * HG_DIM)
        st = st_ref[h]
        parts = []
        for r0 in range(0, c, blk):
            rows = slice(r0, r0 + blk)
            att = jnp.zeros((blk, blk), F32)
            for hs, q_l, k_l in reversed(levels):
                att = jnp.where(blk_xor >= hs, _dot_nt(q_l[rows, sl], k_l[rows, sl]), att)
            parts.append(_dot(att.astype(BF16), v16[rows, sl]))
        for lo, mid, hi, q_w, k_w in wide:
            res = _dot(_dot_nt(q_w[:, sl], k_w[:, sl]).astype(BF16), v16[lo:mid, sl])
            for r0 in range(mid, hi, blk):
                parts[r0 // blk] = parts[r0 // blk] + res[r0 - mid:r0 - mid + blk]
        o = _dot_nt(q_in[:, sl], st.astype(BF16)) + jnp.concatenate(parts, axis=0)
        o = o + jnp.sum(qk[:, sl], axis=-1, keepdims=True) * v[:, sl]
        st_ref[h] = decay[:, sl] * st + _dot_tn(v16[:, sl], k_out[:, sl])
        o = o * lax.rsqrt(jnp.mean(o * o, axis=-1, keepdims=True) + EPS)
        outs.append(o)
    o = jnp.concatenate(outs, axis=-1) * norm_g
    return o * (g * jax.nn.sigmoid(g))


def _mixer_kernel(x_ref, ng_ref, win_ref, wp_ref, ps_ref, cw_ref, cb_ref, wl_ref, bl_ref, lam_ref,
                  lbl_ref, hn_ref, wout_ref, pool0_ref, conv0_ref, h0_ref, s0_ref,
                  y_ref, pool_o_ref, conv_o_ref, h_o_ref, s_o_ref,
                  z_scr, mix_scr, pool_c, conv_c, h_c, st_c, *, layer, pos0, chunk):
    i = pl.program_id(1)
    tt = x_ref.shape[0]

    @pl.when(i == 0)
    def _():
        pool_c[...] = pool0_ref[...]
        conv_c[...] = conv0_ref[...]
        h_c[...] = h0_ref[...]
        for h in range(HG_HEADS):
            st_c[h] = s0_ref[h].T

    x = x_ref[...]
    hn = _rms(x, ng_ref[...]).astype(BF16)
    z_scr[...] = _dot(hn, win_ref[...])
    pos = pos0 + i * tt + lax.broadcasted_iota(jnp.int32, (tt, 1), 0)

    ya, new_pool = _pool_branch(z_scr[:, O_POOL:O_LX], pool_c[...], pos, wp_ref[...], ps_ref[...])
    pool_c[...] = new_pool
    mix_scr[:, 0:POOL_DIM] = ya.astype(BF16)

    yb, new_conv, new_h = _lru_branch(
        z_scr[:, O_LX:O_LG], z_scr[:, O_LG:O_Q], conv_c[...], h_c[SUBLANES - 1:SUBLANES],
        pos, cw_ref[...], cb_ref[...], wl_ref[...], bl_ref[...], lam_ref[...])
    conv_c[...] = new_conv
    h_c[...] = new_h
    mix_scr[:, POOL_DIM:POOL_DIM + LRU_DIM] = yb.astype(BF16)

    logits = lbl_ref[...]
    ex = jnp.exp(logits - jnp.max(logits, axis=0, keepdims=True))
    p = ex / jnp.sum(ex, axis=0, keepdims=True)
    lb = jnp.maximum(jnp.sum(p[0:layer + 1], axis=0, keepdims=True) - p[0:1], 0.0)
    hnorm = hn_ref[...]

    def chunk_body(ci, carry):
        r0 = pl.multiple_of(ci * chunk, chunk)
        rows = pl.ds(r0, chunk)
        yc = _hgrn_chunk(z_scr[rows, O_Q:O_F], z_scr[rows, O_F:O_V], z_scr[rows, O_V:O_G],
                         z_scr[rows, O_G:O_END], lb, hnorm, st_c)
        mix_scr[rows, POOL_DIM + LRU_DIM:MIX_DIM] = yc.astype(BF16)
        return carry

    n_chunks = tt // chunk
    lax.fori_loop(0, n_chunks, chunk_body, 0, unroll=min(HG_UNROLL, n_chunks))

    y_ref[...] = x + _dot(mix_scr[...], wout_ref[...])

    @pl.when(i == pl.num_programs(1) - 1)
    def _():
        pool_o_ref[...] = pool_c[...]
        conv_o_ref[...] = conv_c[...]
        h_o_ref[...] = h_c[...]
        for h in range(HG_HEADS):
            s_o_ref[h] = st_c[h].T


def _block_diag(w):
    g, c, d = w.shape
    eye = jnp.eye(g, dtype=w.dtype)
    return (eye[:, None, :, None] * w[:, :, None, :]).reshape(g * c, g * d)


STATE_SHAPES = ((POOL_ROWS, POOL_DIM), (CONV_ROWS, LRU_DIM), (SUBLANES, LRU_DIM), (HG_HEADS, HG_DIM, HG_DIM))


def _mixer(x, pos0, pool0, conv0, h0, s0, layer, p):
    bsz, t, d = x.shape
    tt = min(MIX_ROWS, t)
    assert t % tt == 0 and tt % POOL_ROWS == 0
    chunk = min(HG_CHUNK, tt)
    assert tt % chunk == 0
    in_dim = p['w_in'].shape[1]
    depth = p['lb_logits'].shape[0]

    states = [jnp.pad(pool0, ((0, 0), (POOL_ROWS - pool0.shape[1], 0), (0, 0))),
              jnp.pad(conv0, ((0, 0), (CONV_ROWS - conv0.shape[1], 0), (0, 0))),
              jnp.pad(h0[:, None, :], ((0, 0), (SUBLANES - 1, 0), (0, 0))), s0]
    state_specs = [pl.BlockSpec((None,) + sh, lambda b, i, sh=sh: (b,) + (0,) * len(sh))
                   for sh in STATE_SHAPES]
    tile = pl.BlockSpec((None, tt, d), lambda b, i: (b, i, 0))
    in_specs = [tile, _const_spec((1, d)), _const_spec((d, in_dim)),
                _const_spec((POOL_DIM, POOL_DIM)), _const_spec((1, POOL_DIM)),
                _const_spec((CONV_WIDTH, LRU_DIM)), _const_spec((1, LRU_DIM)),
                _const_spec((LRU_DIM, 2 * LRU_DIM)), _const_spec((1, 2 * LRU_DIM)), _const_spec((1, LRU_DIM)),
                _const_spec((depth, HG_WIDTH)), _const_spec((1, HG_WIDTH)),
                _const_spec((MIX_DIM, d))] + state_specs
    y, pool_n, conv_n, h_n, s_n = pl.pallas_call(
        functools.partial(_mixer_kernel, layer=layer, pos0=pos0, chunk=chunk),
        grid=(bsz, t // tt),
        in_specs=in_specs,
        out_specs=[tile] + state_specs,
        out_shape=[jax.ShapeDtypeStruct((bsz, t, d), F32)]
                  + [jax.ShapeDtypeStruct((bsz,) + sh, F32) for sh in STATE_SHAPES],
        scratch_shapes=[pltpu.VMEM((tt, in_dim), F32), pltpu.VMEM((tt, MIX_DIM), BF16)]
                       + [pltpu.VMEM(sh, F32) for sh in STATE_SHAPES],
        compiler_params=_params(("parallel", "arbitrary")),
        name="mixer",
    )(x, p['mix_norm'], p['w_in'], p['pool_wbd'], p['pool_scale'], p['conv_w'], p['conv_b'],
      p['lru_wbd'], p['lru_b'], p['lru_lambda'], p['lb_logits'], p['hgrn_norm'], p['w_out'], *states)
    return (y, pool_n[:, POOL_ROWS - POOL_HIST:], conv_n[:, CONV_ROWS - (CONV_WIDTH - 1):],
            h_n[:, SUBLANES - 1], s_n)


def _run(x, pos0, pool, conv, lru, hg, layers, final_norm):
    bsz, t, d = x.shape
    depth = len(layers)
    pools, convs, lrus, hgs = [], [], [], []
    for l, p in enumerate(layers):
        x = _ffn(x.reshape(bsz * t, d), p['ffn1_norm'], p['ffn1_w_gate'], p['ffn1_w_up'],
                 p['ffn1_w_down']).reshape(bsz, t, d)
        x, sp, sc, sl, sh = _mixer(x, pos0, pool[l], conv[l], lru[l], hg[l], l, p)
        x = _ffn(x.reshape(bsz * t, d), p['ffn2_norm'], p['ffn2_w_gate'], p['ffn2_w_up'],
                 p['ffn2_w_down'], final_norm[None] if l == depth - 1 else None).reshape(bsz, t, d)
        pools.append(sp)
        convs.append(sc)
        lrus.append(sl)
        hgs.append(sh)
    return x, jnp.stack(pools), jnp.stack(convs), jnp.stack(lrus), jnp.stack(hgs)


def kernel(x_prompt, x_sample, state_pool, state_conv, state_lru, state_hgrn, ffn1_norm, ffn1_w_gate, ffn1_w_up, ffn1_w_down, mix_norm, w_in, pool_w, pool_scale, conv_w, conv_b, lru_w_a, lru_b_a, lru_w_x, lru_b_x, lru_lambda, hgrn_lb_logits, hgrn_norm, w_out, ffn2_norm, ffn2_w_gate, ffn2_w_up, ffn2_w_down, final_norm):
    depth = w_in.shape[0]
    layers = []
    for l in range(depth):
        layers.append({
            'ffn1_norm': ffn1_norm[l][None], 'ffn1_w_gate': ffn1_w_gate[l].astype(BF16),
            'ffn1_w_up': ffn1_w_up[l].astype(BF16), 'ffn1_w_down': ffn1_w_down[l].astype(BF16),
            'mix_norm': mix_norm[l][None], 'w_in': w_in[l].astype(BF16),
            'pool_wbd': _block_diag(pool_w[l]).astype(BF16), 'pool_scale': pool_scale[l][None],
            'conv_w': conv_w[l], 'conv_b': conv_b[l][None],
            'lru_wbd': jnp.concatenate([_block_diag(lru_w_a[l]), _block_diag(lru_w_x[l])],
                                       axis=1).astype(BF16),
            'lru_b': jnp.concatenate([lru_b_a[l], lru_b_x[l]])[None],
            'lru_lambda': lru_lambda[l][None], 'lb_logits': hgrn_lb_logits,
            'hgrn_norm': hgrn_norm[l][None], 'w_out': w_out[l].astype(BF16),
            'ffn2_norm': ffn2_norm[l][None], 'ffn2_w_gate': ffn2_w_gate[l].astype(BF16),
            'ffn2_w_up': ffn2_w_up[l].astype(BF16), 'ffn2_w_down': ffn2_w_down[l].astype(BF16),
        })
    bp = x_prompt.shape[0]
    dt = x_prompt.dtype
    zero_pool = jnp.zeros((depth, bp) + state_pool.shape[2:], dt)
    zero_conv = jnp.zeros((depth, bp) + state_conv.shape[2:], dt)
    zero_lru = jnp.zeros((depth, bp) + state_lru.shape[2:], dt)
    zero_hgrn = jnp.zeros((depth, bp) + state_hgrn.shape[2:], dt)
    y_p, pool_p, conv_p, lru_p, hgrn_p = _run(x_prompt, 0, zero_pool, zero_conv, zero_lru, zero_hgrn,
                                              layers, final_norm)
    y_s, pool_s, conv_s, lru_s, hgrn_s = _run(x_sample, PAST_LEN, state_pool, state_conv, state_lru,
                                              state_hgrn, layers, final_norm)
    return (y_p, y_s, pool_p, conv_p, lru_p, hgrn_p, pool_s, conv_s, lru_s, hgrn_s)
```

```python
import functools
import math

import jax
import jax.numpy as jnp
from jax import lax
from jax.experimental import pallas as pl
from jax.experimental.pallas import tpu as pltpu

F32 = jnp.float32
BF16 = jnp.bfloat16

EPS = 1e-6
F_MIN = 1e-30
LRU_C = 8.0
LOG2_E = math.log2(math.e)
POOL_WINDOWS = (2, 4, 8, 16)
POOL_GROUP_DIM = 64
POOL_DIM = 256
POOL_HIST = 15
POOL_ROWS = 16
LRU_DIM = 256
CONV_WIDTH = 4
CONV_ROWS = 8
HG_HEADS = 4
HG_DIM = 128
HG_WIDTH = HG_HEADS * HG_DIM
HG_CHUNK = 256
HG_BLOCK = 128
SUBLANES = 8
N_FRONT = POOL_DIM + 2 * LRU_DIM
Z_Q, Z_F, Z_V, Z_G = (slice(i * HG_WIDTH, (i + 1) * HG_WIDTH) for i in range(4))
Z_POOL = slice(4 * HG_WIDTH, 4 * HG_WIDTH + POOL_DIM)
Z_LX = slice(Z_POOL.stop, Z_POOL.stop + LRU_DIM)
Z_LG = slice(Z_LX.stop, Z_LX.stop + LRU_DIM)
MIX_DIM = 1024
PAST_LEN = 2048

VMEM_LIMIT_BYTES = 56 * 1024 * 1024
FFN_ROWS = 1024
CAST_SPLIT = 4
CAST_COLS = 256
MIX_ROWS = 1024


def _rms(x, g):
    ms = jnp.mean(x * x, axis=-1, keepdims=True)
    return x * lax.rsqrt(ms + EPS) * g


def _dot(a, b):
    return jnp.dot(a, b, preferred_element_type=F32)


def _dot_nt(a, b):
    return lax.dot_general(a, b, (((1,), (1,)), ((), ())), preferred_element_type=F32)


def _dot_tn(a, b):
    return lax.dot_general(a, b, (((0,), (0,)), ((), ())), preferred_element_type=F32)


def _shift_rows(x, d):
    return pltpu.roll(x, d, axis=0)


def _tile_view(x):
    return x.reshape(x.shape[0] // SUBLANES, SUBLANES, x.shape[1])


def _row_of_tile(x3, j):
    return jnp.broadcast_to(x3[:, j:j + 1, :], x3.shape)


def _const_spec(shape, single=False):
    index_map = lambda *_: (0,) * len(shape)
    if single:
        return pl.BlockSpec(shape, index_map, pipeline_mode=pl.Buffered(1))
    return pl.BlockSpec(shape, index_map)


def _layer_spec(shape, layer):
    return pl.BlockSpec((None,) + shape, lambda *_: (layer,) + (0,) * len(shape),
                        pipeline_mode=pl.Buffered(1))


def _cast_kernel(w_ref, o_ref):
    o_ref[...] = w_ref[...].astype(o_ref.dtype)


def _to_bf16(w, front_cols=0):
    depth, r, c = w.shape
    if front_cols:
        width = CAST_COLS
        assert c % width == 0 and front_cols % width == 0
        nb, shift = c // width, front_cols // width
        block = (None, r, width)
        grid = (depth, nb)
        src = lambda l, j: (l, 0, jnp.where(j < nb - shift, j + shift, j - (nb - shift)))
        dst = lambda l, j: (l, 0, j)
    else:
        rows = r // CAST_SPLIT
        assert r % CAST_SPLIT == 0 and rows % 16 == 0
        block = (None, rows, c)
        grid = (depth, CAST_SPLIT)
        src = dst = lambda l, i: (l, i, 0)
    return pl.pallas_call(
        _cast_kernel,
        grid=grid,
        in_specs=[pl.BlockSpec(block, src)],
        out_specs=pl.BlockSpec(block, dst),
        out_shape=jax.ShapeDtypeStruct(w.shape, BF16),
        compiler_params=_params(("parallel", "parallel")),
        name="to_bf16",
    )(w)


def _params(semantics):
    return pltpu.CompilerParams(dimension_semantics=semantics, vmem_limit_bytes=VMEM_LIMIT_BYTES)


def _ffn_kernel(*refs, final):
    if final:
        x_ref, g_ref, wg_ref, wu_ref, wd_ref, fg_ref, o_ref = refs
    else:
        x_ref, g_ref, wg_ref, wu_ref, wd_ref, o_ref = refs
    x = x_ref[...]
    h = _rms(x, g_ref[...]).astype(BF16)
    gate = _dot(h, wg_ref[...])
    up = _dot(h, wu_ref[...])
    act = (gate * jax.nn.sigmoid(gate) * up).astype(BF16)
    out = x + 0.5 * _dot(act, wd_ref[...])
    if final:
        out = _rms(out, fg_ref[...])
    o_ref[...] = out


def _ffn(x2d, norm_g, wg, wu, wd, layer, final_g=None):
    n, d = x2d.shape
    dff = wg.shape[2]
    tm = min(FFN_ROWS, n)
    assert n % tm == 0
    final = final_g is not None
    rows = pl.BlockSpec((tm, d), lambda i: (i, 0))
    in_specs = [rows, _const_spec((1, d)),
                _layer_spec((d, dff), layer), _layer_spec((d, dff), layer), _layer_spec((dff, d), layer)]
    args = [x2d, norm_g, wg, wu, wd]
    if final:
        in_specs.append(_const_spec((1, d)))
        args.append(final_g)
    return pl.pallas_call(
        functools.partial(_ffn_kernel, final=final),
        grid=(n // tm,),
        in_specs=in_specs,
        out_specs=rows,
        out_shape=jax.ShapeDtypeStruct((n, d), F32),
        compiler_params=_params(("parallel",)),
        name="ffn_final" if final else "ffn",
    )(*args)


def _pool_branch(u, hist, pos, wp, scale):
    t = u.shape[0]
    ext = jnp.concatenate([hist, u], axis=0)
    w2 = ext + _shift_rows(ext, 1)
    w4 = w2 + _shift_rows(w2, 2)
    w8 = w4 + _shift_rows(w4, 4)
    w16 = w8 + _shift_rows(w8, 8)
    lane = lax.broadcasted_iota(jnp.int32, (1, POOL_DIM), 1)
    group = lane // POOL_GROUP_DIM
    sums = jnp.where(group == 0, w2, jnp.where(group == 1, w4, jnp.where(group == 2, w8, w16)))
    sums = sums[POOL_ROWS:]
    win = jnp.where(group == 0, POOL_WINDOWS[0],
                    jnp.where(group == 1, POOL_WINDOWS[1],
                              jnp.where(group == 2, POOL_WINDOWS[2], POOL_WINDOWS[3])))
    cnt = jnp.minimum(win, pos + 1).astype(F32)
    pooled = sums / cnt - u
    ya = _dot(pooled.astype(BF16), wp) * scale
    return ya, ext[t:]


def _lru_branch(xb, gb, hist, h_prev, pos, conv_w, conv_b, wl, bl, lam):
    t = xb.shape[0]
    ext = jnp.concatenate([hist, xb], axis=0)
    conv = jnp.broadcast_to(conv_b, (t, LRU_DIM))
    for k in range(CONV_WIDTH):
        shifted = _shift_rows(ext, CONV_WIDTH - 1 - k) if k < CONV_WIDTH - 1 else ext
        conv = conv + shifted[CONV_ROWS:] * conv_w[k:k + 1]
    gates = _dot(conv.astype(BF16), wl) + bl
    r = jax.nn.sigmoid(gates[:, :LRU_DIM])
    ig = jax.nn.sigmoid(gates[:, LRU_DIM:])
    a = jnp.exp2((-LRU_C * LOG2_E * jax.nn.softplus(-lam)) * r)
    mult = jnp.sqrt(jnp.maximum(1.0 - a * a, 0.0))
    mult = jnp.where(pos == 0, 1.0, mult)
    bterm = mult * (ig * conv)
    acc_a, acc_b = _tile_view(a), _tile_view(bterm)
    sub = lax.broadcasted_iota(jnp.int32, (1, SUBLANES, 1), 1)
    d = 1
    while d < SUBLANES:
        keep = sub >= d
        prev_a = jnp.where(keep, pltpu.roll(acc_a, d, axis=1), 1.0)
        prev_b = jnp.where(keep, pltpu.roll(acc_b, d, axis=1), 0.0)
        acc_b = acc_a * prev_b + acc_b
        acc_a = acc_a * prev_a
        d *= 2
    tiles = []
    carry = h_prev
    for i in range(t // SUBLANES):
        h_i = acc_a[i] * carry + acc_b[i]
        tiles.append(h_i)
        carry = h_i[SUBLANES - 1:SUBLANES]
    h = jnp.concatenate(tiles, axis=0)
    yb = h * jax.nn.gelu(gb)
    return yb, ext[t:], tiles[-1]


def _hgrn_chunk(q, fz, v, g, lb, norm_g, st_ref):
    c = q.shape[0]
    nt = c // SUBLANES
    one_m_lb = 1.0 - lb
    sig = jax.nn.sigmoid(fz)
    fc = jnp.maximum(lb + one_m_lb * sig, F_MIN)
    log_f = jnp.log2(fc)
    kk = one_m_lb * (1.0 - sig)
    qf = q * jax.nn.sigmoid(q)

    sub = lax.broadcasted_iota(jnp.int32, (1, SUBLANES, 1), 1)
    c3 = _tile_view(log_f)
    d = 1
    while d < SUBLANES:
        c3 = c3 + jnp.where(sub >= d, pltpu.roll(c3, d, axis=1), 0.0)
        d *= 2
    carries = [jnp.zeros((1, 1, HG_WIDTH), F32)]
    for i in range(1, nt):
        carries.append(carries[-1] + c3[i - 1:i, SUBLANES - 1:SUBLANES, :])
    b = (c3 + jnp.concatenate(carries, axis=0)).reshape(c, HG_WIDTH)
    b_last = b[c - 1:c]
    q_in = (qf * jnp.exp2(b)).astype(BF16)
    k_out = (kk * jnp.exp2(b_last - b)).astype(BF16)
    decay = jnp.exp2(b_last)
    v16 = v.astype(BF16)

    blk = min(HG_BLOCK, c)
    wide = []
    hs = c // 2
    while hs >= blk:
        for m in range(c // (2 * hs)):
            lo, mid, hi = m * 2 * hs, m * 2 * hs + hs, (m + 1) * 2 * hs
            ref = b[mid - 1:mid]
            wide.append((lo, mid, hi, (qf[mid:hi] * jnp.exp2(b[mid:hi] - ref)).astype(BF16),
                         (kk[lo:mid] * jnp.exp2(ref - b[lo:mid])).astype(BF16)))
        hs //= 2
    levels = []
    while hs >= SUBLANES:
        q_parts, k_parts = [], []
        zero = jnp.zeros((hs, HG_WIDTH), BF16)
        for m in range(c // (2 * hs)):
            lo, mid, hi = m * 2 * hs, m * 2 * hs + hs, (m + 1) * 2 * hs
            ref = b[mid - 1:mid]
            k_parts += [(kk[lo:mid] * jnp.exp2(ref - b[lo:mid])).astype(BF16), zero]
            q_parts += [zero, (qf[mid:hi] * jnp.exp2(b[mid:hi] - ref)).astype(BF16)]
        levels.append((hs, jnp.concatenate(q_parts, axis=0), jnp.concatenate(k_parts, axis=0)))
        hs //= 2
    qf3, kk3 = _tile_view(qf), _tile_view(kk)
    while hs >= 2:
        if 2 * hs == SUBLANES:
            ref = _row_of_tile(c3, hs - 1)
        else:
            ref = jnp.where(sub < 2 * hs, _row_of_tile(c3, hs - 1), _row_of_tile(c3, 3 * hs - 1))
        e = jnp.exp2(-jnp.abs(c3 - ref))
        upper = (sub & hs) != 0
        q_l = jnp.where(upper, qf3 * e, 0.0).astype(BF16).reshape(c, HG_WIDTH)
        k_l = jnp.where(upper, 0.0, kk3 * e).astype(BF16).reshape(c, HG_WIDTH)
        levels.append((hs, q_l, k_l))
        hs //= 2
    odd = (sub & 1) != 0
    q_l = jnp.where(odd, qf3 * _tile_view(fc), 0.0).astype(BF16).reshape(c, HG_WIDTH)
    k_l = jnp.where(odd, 0.0, kk3).astype(BF16).reshape(c, HG_WIDTH)
    levels.append((1, q_l, k_l))
    qk = qf * kk

    ti = lax.broadcasted_iota(jnp.int32, (blk, blk), 0)
    si = lax.broadcasted_iota(jnp.int32, (blk, blk), 1)
    blk_xor = ti ^ si

    outs = []
    for h in range(HG_HEADS):
        sl = slice(h * HG_DIM, (h + 1) * HG_DIM)
        st = st_ref[h]
        nb = c // blk
        tiles = [[None] * nb for _ in range(nb)]
        for d in range(nb):
            rows = slice(d * blk, (d + 1) * blk)
            att = jnp.zeros((blk, blk), F32)
            for hs, q_l, k_l in reversed(levels):
                att = jnp.where(blk_xor >= hs, _dot_nt(q_l[rows, sl], k_l[rows, sl]), att)
            tiles[d][d] = att.astype(BF16)
        for lo, mid, hi, q_w, k_w in wide:
            a_w = _dot_nt(q_w[:, sl], k_w[:, sl]).astype(BF16)
            for r in range(mid, hi, blk):
                for s0 in range(lo, mid, blk):
                    tiles[r // blk][s0 // blk] = a_w[r - mid:r - mid + blk, s0 - lo:s0 - lo + blk]
        zero_tile = jnp.zeros((blk, blk), BF16)
        att = jnp.concatenate(
            [jnp.concatenate([t if t is not None else zero_tile for t in row], axis=1) for row in tiles],
            axis=0)
        o = _dot_nt(q_in[:, sl], st.astype(BF16)) + _dot(att, v16[:, sl])
        o = o + jnp.sum(qk[:, sl], axis=-1, keepdims=True) * v[:, sl]
        st_ref[h] = decay[:, sl] * st + _dot_tn(v16[:, sl], k_out[:, sl])
        o = o * lax.rsqrt(jnp.mean(o * o, axis=-1, keepdims=True) + EPS)
        outs.append(o)
    o = jnp.concatenate(outs, axis=-1) * norm_g
    return o * (g * jax.nn.sigmoid(g))


def _mixer_kernel(x_ref, ng_ref, win_ref, wp_ref, ps_ref, cw_ref, cb_ref, wl_ref, bl_ref, lam_ref,
                  lbl_ref, hn_ref, wout_ref, pool0_ref, conv0_ref, h0_ref, s0_ref,
                  y_ref, pool_o_ref, conv_o_ref, h_o_ref, s_o_ref,
                  z_scr, mix_scr, pool_c, conv_c, h_c, st_c, *, layer, pos0, chunk):
    i = pl.program_id(1)
    tt = x_ref.shape[0]

    @pl.when(i == 0)
    def _():
        pool_c[...] = pool0_ref[...]
        conv_c[...] = conv0_ref[...]
        h_c[...] = h0_ref[...]
        for h in range(HG_HEADS):
            st_c[h] = s0_ref[h].T

    x = x_ref[...]
    hn = _rms(x, ng_ref[...]).astype(BF16)
    pos = pos0 + i * tt + lax.broadcasted_iota(jnp.int32, (tt, 1), 0)
    logits = lbl_ref[...]
    ex = jnp.exp(logits - jnp.max(logits, axis=0, keepdims=True))
    p = ex / jnp.sum(ex, axis=0, keepdims=True)
    lb = jnp.maximum(jnp.sum(p[0:layer + 1], axis=0, keepdims=True) - p[0:1], 0.0)
    hnorm = hn_ref[...]

    def hgrn(r0):
        rows = slice(r0, r0 + chunk)
        yc = _hgrn_chunk(z_scr[rows, Z_Q], z_scr[rows, Z_F], z_scr[rows, Z_V], z_scr[rows, Z_G],
                         lb, hnorm, st_c)
        mix_scr[rows, POOL_DIM + LRU_DIM:MIX_DIM] = yc.astype(BF16)

    z_scr[:, :Z_POOL.start] = _dot(hn, win_ref[:, :Z_POOL.start])
    hgrn(0)
    z_scr[:, Z_POOL.start:] = _dot(hn, win_ref[:, Z_POOL.start:])

    ya, new_pool = _pool_branch(z_scr[:, Z_POOL], pool_c[...], pos, wp_ref[...], ps_ref[...])
    pool_c[...] = new_pool
    mix_scr[:, 0:POOL_DIM] = ya.astype(BF16)

    yb, new_conv, new_h = _lru_branch(
        z_scr[:, Z_LX], z_scr[:, Z_LG], conv_c[...], h_c[SUBLANES - 1:SUBLANES],
        pos, cw_ref[...], cb_ref[...], wl_ref[...], bl_ref[...], lam_ref[...])
    conv_c[...] = new_conv
    h_c[...] = new_h
    mix_scr[:, POOL_DIM:POOL_DIM + LRU_DIM] = yb.astype(BF16)

    for r0 in range(chunk, tt, chunk):
        hgrn(r0)
    y_ref[...] = x + _dot(mix_scr[...], wout_ref[...])

    @pl.when(i == pl.num_programs(1) - 1)
    def _():
        pool_o_ref[...] = pool_c[...]
        conv_o_ref[...] = conv_c[...]
        h_o_ref[...] = h_c[...]
        for h in range(HG_HEADS):
            s_o_ref[h] = st_c[h].T


def _block_diag(w):
    g, c, d = w.shape
    eye = jnp.eye(g, dtype=w.dtype)
    return (eye[:, None, :, None] * w[:, :, None, :]).reshape(g * c, g * d)


STATE_SHAPES = ((POOL_ROWS, POOL_DIM), (CONV_ROWS, LRU_DIM), (SUBLANES, LRU_DIM), (HG_HEADS, HG_DIM, HG_DIM))


def _mixer(x, pos0, pool0, conv0, h0, s0, layer, p):
    bsz, t, d = x.shape
    tt = min(MIX_ROWS, t)
    assert t % tt == 0 and tt % POOL_ROWS == 0
    chunk = min(HG_CHUNK, tt)
    assert tt % chunk == 0
    in_dim = p['w_in'].shape[2]
    depth = p['lb_logits'].shape[0]

    states = [jnp.pad(pool0, ((0, 0), (POOL_ROWS - pool0.shape[1], 0), (0, 0))),
              jnp.pad(conv0, ((0, 0), (CONV_ROWS - conv0.shape[1], 0), (0, 0))),
              jnp.pad(h0[:, None, :], ((0, 0), (SUBLANES - 1, 0), (0, 0))), s0]
    state_specs = [pl.BlockSpec((None,) + sh, lambda b, i, sh=sh: (b,) + (0,) * len(sh))
                   for sh in STATE_SHAPES]
    tile = pl.BlockSpec((None, tt, d), lambda b, i: (b, i, 0))
    in_specs = [tile, _const_spec((1, d)), _layer_spec((d, in_dim), layer),
                _const_spec((POOL_DIM, POOL_DIM)), _const_spec((1, POOL_DIM)),
                _const_spec((CONV_WIDTH, LRU_DIM)), _const_spec((1, LRU_DIM)),
                _const_spec((LRU_DIM, 2 * LRU_DIM)), _const_spec((1, 2 * LRU_DIM)), _const_spec((1, LRU_DIM)),
                _const_spec((depth, HG_WIDTH)), _const_spec((1, HG_WIDTH)),
                _layer_spec((MIX_DIM, d), layer)] + state_specs
    y, pool_n, conv_n, h_n, s_n = pl.pallas_call(
        functools.partial(_mixer_kernel, layer=layer, pos0=pos0, chunk=chunk),
        grid=(bsz, t // tt),
        in_specs=in_specs,
        out_specs=[tile] + state_specs,
        out_shape=[jax.ShapeDtypeStruct((bsz, t, d), F32)]
                  + [jax.ShapeDtypeStruct((bsz,) + sh, F32) for sh in STATE_SHAPES],
        scratch_shapes=[pltpu.VMEM((tt, in_dim), F32), pltpu.VMEM((tt, MIX_DIM), BF16)]
                       + [pltpu.VMEM(sh, F32) for sh in STATE_SHAPES],
        compiler_params=_params(("parallel", "arbitrary")),
        name="mixer",
    )(x, p['mix_norm'], p['w_in'], p['pool_wbd'], p['pool_scale'], p['conv_w'], p['conv_b'],
      p['lru_wbd'], p['lru_b'], p['lru_lambda'], p['lb_logits'], p['hgrn_norm'], p['w_out'], *states)
    return (y, pool_n[:, POOL_ROWS - POOL_HIST:], conv_n[:, CONV_ROWS - (CONV_WIDTH - 1):],
            h_n[:, SUBLANES - 1], s_n)


def _run(x, pos0, pool, conv, lru, hg, layers, final_norm):
    bsz, t, d = x.shape
    depth = len(layers)
    pools, convs, lrus, hgs = [], [], [], []
    for l, p in enumerate(layers):
        x = _ffn(x.reshape(bsz * t, d), p['ffn1_norm'], p['ffn1_w_gate'], p['ffn1_w_up'],
                 p['ffn1_w_down'], l).reshape(bsz, t, d)
        x, sp, sc, sl, sh = _mixer(x, pos0, pool[l], conv[l], lru[l], hg[l], l, p)
        x = _ffn(x.reshape(bsz * t, d), p['ffn2_norm'], p['ffn2_w_gate'], p['ffn2_w_up'],
                 p['ffn2_w_down'], l, final_norm[None] if l == depth - 1 else None).reshape(bsz, t, d)
        pools.append(sp)
        convs.append(sc)
        lrus.append(sl)
        hgs.append(sh)
    return x, jnp.stack(pools), jnp.stack(convs), jnp.stack(lrus), jnp.stack(hgs)


def kernel(x_prompt, x_sample, state_pool, state_conv, state_lru, state_hgrn, ffn1_norm, ffn1_w_gate, ffn1_w_up, ffn1_w_down, mix_norm, w_in, pool_w, pool_scale, conv_w, conv_b, lru_w_a, lru_b_a, lru_w_x, lru_b_x, lru_lambda, hgrn_lb_logits, hgrn_norm, w_out, ffn2_norm, ffn2_w_gate, ffn2_w_up, ffn2_w_down, final_norm):
    depth = w_in.shape[0]
    big = {'ffn1_w_gate': _to_bf16(ffn1_w_gate), 'ffn1_w_up': _to_bf16(ffn1_w_up),
           'ffn1_w_down': _to_bf16(ffn1_w_down), 'w_in': _to_bf16(w_in, N_FRONT), 'w_out': _to_bf16(w_out),
           'ffn2_w_gate': _to_bf16(ffn2_w_gate), 'ffn2_w_up': _to_bf16(ffn2_w_up),
           'ffn2_w_down': _to_bf16(ffn2_w_down)}
    layers = []
    for l in range(depth):
        layers.append({
            **big, 'ffn1_norm': ffn1_norm[l][None], 'mix_norm': mix_norm[l][None],
            'pool_wbd': _block_diag(pool_w[l]).astype(BF16), 'pool_scale': pool_scale[l][None],
            'conv_w': conv_w[l], 'conv_b': conv_b[l][None],
            'lru_wbd': jnp.concatenate([_block_diag(lru_w_a[l]), _block_diag(lru_w_x[l])],
                                       axis=1).astype(BF16),
            'lru_b': jnp.concatenate([lru_b_a[l], lru_b_x[l]])[None],
            'lru_lambda': lru_lambda[l][None], 'lb_logits': hgrn_lb_logits,
            'hgrn_norm': hgrn_norm[l][None], 'ffn2_norm': ffn2_norm[l][None],
        })
    bp = x_prompt.shape[0]
    dt = x_prompt.dtype
    zero_pool = jnp.zeros((depth, bp) + state_pool.shape[2:], dt)
    zero_conv = jnp.zeros((depth, bp) + state_conv.shape[2:], dt)
    zero_lru = jnp.zeros((depth, bp) + state_lru.shape[2:], dt)
    zero_hgrn = jnp.zeros((depth, bp) + state_hgrn.shape[2:], dt)
    y_p, pool_p, conv_p, lru_p, hgrn_p = _run(x_prompt, 0, zero_pool, zero_conv, zero_lru, zero_hgrn,
                                              layers, final_norm)
    y_s, pool_s, conv_s, lru_s, hgrn_s = _run(x_sample, PAST_LEN, state_pool, state_conv, state_lru,
                                              state_hgrn, layers, final_norm)
    return (y_p, y_s, pool_p, conv_p, lru_p, hgrn_p, pool_s, conv_s, lru_s, hgrn_s)
```

```python
import functools
import math

import jax
import jax.numpy as jnp
from jax import lax
from jax.experimental import pallas as pl
from jax.experimental.pallas import tpu as pltpu

F32 = jnp.float32
BF16 = jnp.bfloat16

EPS = 1e-6
F_MIN = 1e-30
LRU_C = 8.0
LOG2_E = math.log2(math.e)
POOL_WINDOWS = (2, 4, 8, 16)
POOL_GROUP_DIM = 64
POOL_DIM = 256
POOL_HIST = 15
POOL_ROWS = 16
LRU_DIM = 256
CONV_WIDTH = 4
CONV_ROWS = 8
HG_HEADS = 4
HG_DIM = 128
HG_WIDTH = HG_HEADS * HG_DIM
HG_CHUNK = 256
HG_BLOCK = 128
SUBLANES = 8
N_FRONT = POOL_DIM + 2 * LRU_DIM
Z_Q, Z_F, Z_V, Z_G = (slice(i * HG_WIDTH, (i + 1) * HG_WIDTH) for i in range(4))
Z_POOL = slice(4 * HG_WIDTH, 4 * HG_WIDTH + POOL_DIM)
Z_LX = slice(Z_POOL.stop, Z_POOL.stop + LRU_DIM)
Z_LG = slice(Z_LX.stop, Z_LX.stop + LRU_DIM)
MIX_DIM = 1024
PAST_LEN = 2048

VMEM_LIMIT_BYTES = 56 * 1024 * 1024
FFN_ROWS = 1024
CAST_SPLIT = 4
CAST_COLS = 256
MIX_ROWS = 1024


def _rms(x, g):
    ms = jnp.mean(x * x, axis=-1, keepdims=True)
    return x * lax.rsqrt(ms + EPS) * g


def _dot(a, b):
    return jnp.dot(a, b, preferred_element_type=F32)


def _dot_nt(a, b):
    return lax.dot_general(a, b, (((1,), (1,)), ((), ())), preferred_element_type=F32)


def _dot_tn(a, b):
    return lax.dot_general(a, b, (((0,), (0,)), ((), ())), preferred_element_type=F32)


def _shift_rows(x, d):
    return pltpu.roll(x, d, axis=0)


def _tile_view(x):
    return x.reshape(x.shape[0] // SUBLANES, SUBLANES, x.shape[1])


def _row_of_tile(x3, j):
    return jnp.broadcast_to(x3[:, j:j + 1, :], x3.shape)


def _const_spec(shape, single=False):
    index_map = lambda *_: (0,) * len(shape)
    if single:
        return pl.BlockSpec(shape, index_map, pipeline_mode=pl.Buffered(1))
    return pl.BlockSpec(shape, index_map)


def _layer_spec(shape, layer):
    return pl.BlockSpec((None,) + shape, lambda *_: (layer,) + (0,) * len(shape),
                        pipeline_mode=pl.Buffered(1))


def _cast_kernel(w_ref, o_ref):
    o_ref[...] = w_ref[...].astype(o_ref.dtype)


def _to_bf16(w, front_cols=0):
    depth, r, c = w.shape
    if front_cols:
        width = CAST_COLS
        assert c % width == 0 and front_cols % width == 0
        nb, shift = c // width, front_cols // width
        block = (None, r, width)
        grid = (depth, nb)
        src = lambda l, j: (l, 0, jnp.where(j < nb - shift, j + shift, j - (nb - shift)))
        dst = lambda l, j: (l, 0, j)
    else:
        rows = r // CAST_SPLIT
        assert r % CAST_SPLIT == 0 and rows % 16 == 0
        block = (None, rows, c)
        grid = (depth, CAST_SPLIT)
        src = dst = lambda l, i: (l, i, 0)
    return pl.pallas_call(
        _cast_kernel,
        grid=grid,
        in_specs=[pl.BlockSpec(block, src)],
        out_specs=pl.BlockSpec(block, dst),
        out_shape=jax.ShapeDtypeStruct(w.shape, BF16),
        compiler_params=_params(("parallel", "parallel")),
        name="to_bf16",
    )(w)


def _params(semantics):
    return pltpu.CompilerParams(dimension_semantics=semantics, vmem_limit_bytes=VMEM_LIMIT_BYTES)


def _ffn_kernel(*refs, final):
    if final:
        x_ref, g_ref, wg_ref, wu_ref, wd_ref, fg_ref, o_ref = refs
    else:
        x_ref, g_ref, wg_ref, wu_ref, wd_ref, o_ref = refs
    x = x_ref[...]
    h = _rms(x, g_ref[...]).astype(BF16)
    gate = _dot(h, wg_ref[...])
    up = _dot(h, wu_ref[...])
    act = (gate * jax.nn.sigmoid(gate) * up).astype(BF16)
    out = x + 0.5 * _dot(act, wd_ref[...])
    if final:
        out = _rms(out, fg_ref[...])
    o_ref[...] = out


def _ffn(x2d, norm_g, wg, wu, wd, layer, final_g=None):
    n, d = x2d.shape
    dff = wg.shape[2]
    tm = min(FFN_ROWS, n)
    assert n % tm == 0
    final = final_g is not None
    rows = pl.BlockSpec((tm, d), lambda i: (i, 0))
    in_specs = [rows, _const_spec((1, d)),
                _layer_spec((d, dff), layer), _layer_spec((d, dff), layer), _layer_spec((dff, d), layer)]
    args = [x2d, norm_g, wg, wu, wd]
    if final:
        in_specs.append(_const_spec((1, d)))
        args.append(final_g)
    return pl.pallas_call(
        functools.partial(_ffn_kernel, final=final),
        grid=(n // tm,),
        in_specs=in_specs,
        out_specs=rows,
        out_shape=jax.ShapeDtypeStruct((n, d), F32),
        compiler_params=_params(("parallel",)),
        name="ffn_final" if final else "ffn",
    )(*args)


def _pool_branch(u, hist, pos, wp, scale):
    t = u.shape[0]
    ext = jnp.concatenate([hist, u], axis=0)
    w2 = ext + _shift_rows(ext, 1)
    w4 = w2 + _shift_rows(w2, 2)
    w8 = w4 + _shift_rows(w4, 4)
    w16 = w8 + _shift_rows(w8, 8)
    lane = lax.broadcasted_iota(jnp.int32, (1, POOL_DIM), 1)
    group = lane // POOL_GROUP_DIM
    sums = jnp.where(group == 0, w2, jnp.where(group == 1, w4, jnp.where(group == 2, w8, w16)))
    sums = sums[POOL_ROWS:]
    win = jnp.where(group == 0, POOL_WINDOWS[0],
                    jnp.where(group == 1, POOL_WINDOWS[1],
                              jnp.where(group == 2, POOL_WINDOWS[2], POOL_WINDOWS[3])))
    cnt = jnp.minimum(win, pos + 1).astype(F32)
    pooled = sums / cnt - u
    ya = _dot(pooled.astype(BF16), wp) * scale
    return ya, ext[t:]


def _lru_branch(xb, gb, hist, h_prev, pos, conv_w, conv_b, wl, bl, lam):
    t = xb.shape[0]
    ext = jnp.concatenate([hist, xb], axis=0)
    conv = jnp.broadcast_to(conv_b, (t, LRU_DIM))
    for k in range(CONV_WIDTH):
        shifted = _shift_rows(ext, CONV_WIDTH - 1 - k) if k < CONV_WIDTH - 1 else ext
        conv = conv + shifted[CONV_ROWS:] * conv_w[k:k + 1]
    gates = _dot(conv.astype(BF16), wl) + bl
    r = jax.nn.sigmoid(gates[:, :LRU_DIM])
    ig = jax.nn.sigmoid(gates[:, LRU_DIM:])
    a = jnp.exp2((-LRU_C * LOG2_E * jax.nn.softplus(-lam)) * r)
    mult = jnp.sqrt(jnp.maximum(1.0 - a * a, 0.0))
    mult = jnp.where(pos == 0, 1.0, mult)
    bterm = mult * (ig * conv)
    acc_a, acc_b = _tile_view(a), _tile_view(bterm)
    sub = lax.broadcasted_iota(jnp.int32, (1, SUBLANES, 1), 1)
    d = 1
    while d < SUBLANES:
        keep = sub >= d
        prev_a = jnp.where(keep, pltpu.roll(acc_a, d, axis=1), 1.0)
        prev_b = jnp.where(keep, pltpu.roll(acc_b, d, axis=1), 0.0)
        acc_b = acc_a * prev_b + acc_b
        acc_a = acc_a * prev_a
        d *= 2
    tiles = []
    carry = h_prev
    for i in range(t // SUBLANES):
        h_i = acc_a[i] * carry + acc_b[i]
        tiles.append(h_i)
        carry = h_i[SUBLANES - 1:SUBLANES]
    h = jnp.concatenate(tiles, axis=0)
    yb = h * jax.nn.gelu(gb)
    return yb, ext[t:], tiles[-1]


def _hgrn_chunk(q, fz, v, g, lb, norm_g, st_ref):
    c = q.shape[0]
    nt = c // SUBLANES
    one_m_lb = 1.0 - lb
    sig = jax.nn.sigmoid(fz)
    fc = jnp.maximum(lb + one_m_lb * sig, F_MIN)
    log_f = jnp.log2(fc)
    kk = one_m_lb * (1.0 - sig)
    qf = q * jax.nn.sigmoid(q)

    sub = lax.broadcasted_iota(jnp.int32, (1, SUBLANES, 1), 1)
    c3 = _tile_view(log_f)
    d = 1
    while d < SUBLANES:
        c3 = c3 + jnp.where(sub >= d, pltpu.roll(c3, d, axis=1), 0.0)
        d *= 2
    carries = [jnp.zeros((1, 1, HG_WIDTH), F32)]
    for i in range(1, nt):
        carries.append(carries[-1] + c3[i - 1:i, SUBLANES - 1:SUBLANES, :])
    b = (c3 + jnp.concatenate(carries, axis=0)).reshape(c, HG_WIDTH)
    b_last = b[c - 1:c]
    q_in = (qf * jnp.exp2(b)).astype(BF16)
    k_out = (kk * jnp.exp2(b_last - b)).astype(BF16)
    decay = jnp.exp2(b_last)
    v16 = v.astype(BF16)

    blk = min(HG_BLOCK, c)
    wide = []
    hs = c // 2
    while hs >= blk:
        for m in range(c // (2 * hs)):
            lo, mid, hi = m * 2 * hs, m * 2 * hs + hs, (m + 1) * 2 * hs
            ref = b[mid - 1:mid]
            wide.append((lo, mid, hi, (qf[mid:hi] * jnp.exp2(b[mid:hi] - ref)).astype(BF16),
                         (kk[lo:mid] * jnp.exp2(ref - b[lo:mid])).astype(BF16)))
        hs //= 2
    levels = []
    while hs >= SUBLANES:
        q_parts, k_parts = [], []
        zero = jnp.zeros((hs, HG_WIDTH), BF16)
        for m in range(c // (2 * hs)):
            lo, mid, hi = m * 2 * hs, m * 2 * hs + hs, (m + 1) * 2 * hs
            ref = b[mid - 1:mid]
            k_parts += [(kk[lo:mid] * jnp.exp2(ref - b[lo:mid])).astype(BF16), zero]
            q_parts += [zero, (qf[mid:hi] * jnp.exp2(b[mid:hi] - ref)).astype(BF16)]
        levels.append((hs, jnp.concatenate(q_parts, axis=0), jnp.concatenate(k_parts, axis=0)))
        hs //= 2
    qf3, kk3 = _tile_view(qf), _tile_view(kk)
    while hs >= 2:
        if 2 * hs == SUBLANES:
            ref = _row_of_tile(c3, hs - 1)
        else:
            ref = jnp.where(sub < 2 * hs, _row_of_tile(c3, hs - 1), _row_of_tile(c3, 3 * hs - 1))
        e = jnp.exp2(-jnp.abs(c3 - ref))
        upper = (sub & hs) != 0
        q_l = jnp.where(upper, qf3 * e, 0.0).astype(BF16).reshape(c, HG_WIDTH)
        k_l = jnp.where(upper, 0.0, kk3 * e).astype(BF16).reshape(c, HG_WIDTH)
        levels.append((hs, q_l, k_l))
        hs //= 2
    odd = (sub & 1) != 0
    q_l = jnp.where(odd, qf3 * _tile_view(fc), 0.0).astype(BF16).reshape(c, HG_WIDTH)
    k_l = jnp.where(odd, 0.0, kk3).astype(BF16).reshape(c, HG_WIDTH)
    levels.append((1, q_l, k_l))
    qk = qf * kk

    ti = lax.broadcasted_iota(jnp.int32, (blk, blk), 0)
    si = lax.broadcasted_iota(jnp.int32, (blk, blk), 1)
    blk_xor = ti ^ si

    outs = []
    for h in range(HG_HEADS):
        sl = slice(h * HG_DIM, (h + 1) * HG_DIM)
        st = st_ref[h]
        nb = c // blk
        tiles = [[None] * nb for _ in range(nb)]
        for d in range(nb):
            rows = slice(d * blk, (d + 1) * blk)
            att = jnp.zeros((blk, blk), F32)
            for hs, q_l, k_l in reversed(levels):
                att = jnp.where(blk_xor >= hs, _dot_nt(q_l[rows, sl], k_l[rows, sl]), att)
            tiles[d][d] = att.astype(BF16)
        for lo, mid, hi, q_w, k_w in wide:
            a_w = _dot_nt(q_w[:, sl], k_w[:, sl]).astype(BF16)
            for r in range(mid, hi, blk):
                for s0 in range(lo, mid, blk):
                    tiles[r // blk][s0 // blk] = a_w[r - mid:r - mid + blk, s0 - lo:s0 - lo + blk]
        zero_tile = jnp.zeros((blk, blk), BF16)
        att = jnp.concatenate(
            [jnp.concatenate([t if t is not None else zero_tile for t in row], axis=1) for row in tiles],
            axis=0)
        o = _dot_nt(q_in[:, sl], st.astype(BF16)) + _dot(att, v16[:, sl])
        o = o + jnp.sum(qk[:, sl], axis=-1, keepdims=True) * v[:, sl]
        st_ref[h] = decay[:, sl] * st + _dot_tn(v16[:, sl], k_out[:, sl])
        o = o * lax.rsqrt(jnp.mean(o * o, axis=-1, keepdims=True) + EPS)
        outs.append(o)
    o = jnp.concatenate(outs, axis=-1) * norm_g
    return o * (g * jax.nn.sigmoid(g))


def _lower_bound(lbl_ref, layer):
    logits = lbl_ref[...]
    ex = jnp.exp(logits - jnp.max(logits, axis=0, keepdims=True))
    p = ex / jnp.sum(ex, axis=0, keepdims=True)
    return jnp.maximum(jnp.sum(p[0:layer + 1], axis=0, keepdims=True) - p[0:1], 0.0)


def _mixer_kernel(x_ref, ng_ref, win_ref, wp_ref, ps_ref, cw_ref, cb_ref, wl_ref, bl_ref, lam_ref,
                  lbl_ref, hn_ref, wout_ref, pool0_ref, conv0_ref, h0_ref, s0_ref,
                  y_ref, pool_o_ref, conv_o_ref, h_o_ref, s_o_ref,
                  z_scr, mix_scr, pool_c, conv_c, h_c, st_c, *, layer, pos0, chunk):
    i = pl.program_id(1)
    tt = x_ref.shape[0]

    @pl.when(i == 0)
    def _():
        pool_c[...] = pool0_ref[...]
        conv_c[...] = conv0_ref[...]
        h_c[...] = h0_ref[...]
        for h in range(HG_HEADS):
            st_c[h] = s0_ref[h].T

    x = x_ref[...]
    hn = _rms(x, ng_ref[...]).astype(BF16)
    pos = pos0 + i * tt + lax.broadcasted_iota(jnp.int32, (tt, 1), 0)
    lb = _lower_bound(lbl_ref, layer)
    hnorm = hn_ref[...]

    def hgrn(r0):
        rows = slice(r0, r0 + chunk)
        yc = _hgrn_chunk(z_scr[rows, Z_Q], z_scr[rows, Z_F], z_scr[rows, Z_V], z_scr[rows, Z_G],
                         lb, hnorm, st_c)
        mix_scr[rows, POOL_DIM + LRU_DIM:MIX_DIM] = yc.astype(BF16)

    z_scr[:, :Z_POOL.start] = _dot(hn, win_ref[:, :Z_POOL.start])
    hgrn(0)
    z_scr[:, Z_POOL.start:] = _dot(hn, win_ref[:, Z_POOL.start:])

    ya, new_pool = _pool_branch(z_scr[:, Z_POOL], pool_c[...], pos, wp_ref[...], ps_ref[...])
    pool_c[...] = new_pool
    mix_scr[:, 0:POOL_DIM] = ya.astype(BF16)

    yb, new_conv, new_h = _lru_branch(
        z_scr[:, Z_LX], z_scr[:, Z_LG], conv_c[...], h_c[SUBLANES - 1:SUBLANES],
        pos, cw_ref[...], cb_ref[...], wl_ref[...], bl_ref[...], lam_ref[...])
    conv_c[...] = new_conv
    h_c[...] = new_h
    mix_scr[:, POOL_DIM:POOL_DIM + LRU_DIM] = yb.astype(BF16)

    for r0 in range(chunk, tt, chunk):
        hgrn(r0)
    y_ref[...] = x + _dot(mix_scr[...], wout_ref[...])

    @pl.when(i == pl.num_programs(1) - 1)
    def _():
        pool_o_ref[...] = pool_c[...]
        conv_o_ref[...] = conv_c[...]
        h_o_ref[...] = h_c[...]
        for h in range(HG_HEADS):
            s_o_ref[h] = st_c[h].T


class _StateIO:
    def __init__(self, src, dst, seq):
        self.src, self.dst, self.seq = src, dst, seq

    def __getitem__(self, h):
        return self.src[self.seq, h].T

    def __setitem__(self, h, val):
        self.dst[self.seq, h] = val.T


def _mixer_group_kernel(x_ref, ng_ref, win_ref, wp_ref, ps_ref, cw_ref, cb_ref, wl_ref, bl_ref, lam_ref,
                        lbl_ref, hn_ref, wout_ref, pool0_ref, conv0_ref, h0_ref, s0_ref,
                        y_ref, pool_o_ref, conv_o_ref, h_o_ref, s_o_ref, z_scr, mix_scr, *, layer, pos0):
    g, t, d = x_ref.shape
    x = x_ref[...].reshape(g * t, d)
    hn = _rms(x, ng_ref[...]).astype(BF16)
    z_scr[...] = _dot(hn, win_ref[...])
    pos = pos0 + lax.broadcasted_iota(jnp.int32, (t, 1), 0)
    lb = _lower_bound(lbl_ref, layer)
    hnorm = hn_ref[...]
    for s in range(g):
        rows = slice(s * t, (s + 1) * t)
        ya, new_pool = _pool_branch(z_scr[rows, Z_POOL], pool0_ref[s], pos, wp_ref[...], ps_ref[...])
        pool_o_ref[s] = new_pool
        mix_scr[rows, 0:POOL_DIM] = ya.astype(BF16)
        yb, new_conv, new_h = _lru_branch(
            z_scr[rows, Z_LX], z_scr[rows, Z_LG], conv0_ref[s], h0_ref[s, SUBLANES - 1:SUBLANES],
            pos, cw_ref[...], cb_ref[...], wl_ref[...], bl_ref[...], lam_ref[...])
        conv_o_ref[s] = new_conv
        h_o_ref[s] = new_h
        mix_scr[rows, POOL_DIM:POOL_DIM + LRU_DIM] = yb.astype(BF16)
        yc = _hgrn_chunk(z_scr[rows, Z_Q], z_scr[rows, Z_F], z_scr[rows, Z_V], z_scr[rows, Z_G],
                         lb, hnorm, _StateIO(s0_ref, s_o_ref, s))
        mix_scr[rows, POOL_DIM + LRU_DIM:MIX_DIM] = yc.astype(BF16)
    y_ref[...] = (x + _dot(mix_scr[...], wout_ref[...])).reshape(g, t, d)


def _block_diag(w):
    g, c, d = w.shape
    eye = jnp.eye(g, dtype=w.dtype)
    return (eye[:, None, :, None] * w[:, :, None, :]).reshape(g * c, g * d)


STATE_SHAPES = ((POOL_ROWS, POOL_DIM), (CONV_ROWS, LRU_DIM), (SUBLANES, LRU_DIM), (HG_HEADS, HG_DIM, HG_DIM))


def _mixer(x, pos0, pool0, conv0, h0, s0, layer, p):
    bsz, t, d = x.shape
    tt = min(MIX_ROWS, t)
    assert t % tt == 0 and tt % POOL_ROWS == 0
    chunk = min(HG_CHUNK, tt)
    assert tt % chunk == 0
    in_dim = p['w_in'].shape[2]
    depth = p['lb_logits'].shape[0]

    group = min(bsz, MIX_ROWS // t) if t <= HG_CHUNK else 1
    states = [jnp.pad(pool0, ((0, 0), (POOL_ROWS - pool0.shape[1], 0), (0, 0))),
              jnp.pad(conv0, ((0, 0), (CONV_ROWS - conv0.shape[1], 0), (0, 0))),
              jnp.pad(h0[:, None, :], ((0, 0), (SUBLANES - 1, 0), (0, 0))), s0]
    if group > 1:
        assert bsz % group == 0
        kern = functools.partial(_mixer_group_kernel, layer=layer, pos0=pos0)
        state_specs = [pl.BlockSpec((group,) + sh, lambda b, i, sh=sh: (b,) + (0,) * len(sh))
                       for sh in STATE_SHAPES]
        tile = pl.BlockSpec((group, t, d), lambda b, i: (b, 0, 0))
        grid = (bsz // group, 1)
        scratch = [pltpu.VMEM((group * t, in_dim), F32), pltpu.VMEM((group * t, MIX_DIM), BF16)]
    else:
        kern = functools.partial(_mixer_kernel, layer=layer, pos0=pos0, chunk=chunk)
        state_specs = [pl.BlockSpec((None,) + sh, lambda b, i, sh=sh: (b,) + (0,) * len(sh))
                       for sh in STATE_SHAPES]
        tile = pl.BlockSpec((None, tt, d), lambda b, i: (b, i, 0))
        grid = (bsz, t // tt)
        scratch = ([pltpu.VMEM((tt, in_dim), F32), pltpu.VMEM((tt, MIX_DIM), BF16)]
                   + [pltpu.VMEM(sh, F32) for sh in STATE_SHAPES])
    in_specs = [tile, _const_spec((1, d)), _layer_spec((d, in_dim), layer),
                _const_spec((POOL_DIM, POOL_DIM)), _const_spec((1, POOL_DIM)),
                _const_spec((CONV_WIDTH, LRU_DIM)), _const_spec((1, LRU_DIM)),
                _const_spec((LRU_DIM, 2 * LRU_DIM)), _const_spec((1, 2 * LRU_DIM)), _const_spec((1, LRU_DIM)),
                _const_spec((depth, HG_WIDTH)), _const_spec((1, HG_WIDTH)),
                _layer_spec((MIX_DIM, d), layer)] + state_specs
    y, pool_n, conv_n, h_n, s_n = pl.pallas_call(
        kern,
        grid=grid,
        in_specs=in_specs,
        out_specs=[tile] + state_specs,
        out_shape=[jax.ShapeDtypeStruct((bsz, t, d), F32)]
                  + [jax.ShapeDtypeStruct((bsz,) + sh, F32) for sh in STATE_SHAPES],
        scratch_shapes=scratch,
        compiler_params=_params(("parallel", "arbitrary")),
        name="mixer",
    )(x, p['mix_norm'], p['w_in'], p['pool_wbd'], p['pool_scale'], p['conv_w'], p['conv_b'],
      p['lru_wbd'], p['lru_b'], p['lru_lambda'], p['lb_logits'], p['hgrn_norm'], p['w_out'], *states)
    return (y, pool_n[:, POOL_ROWS - POOL_HIST:], conv_n[:, CONV_ROWS - (CONV_WIDTH - 1):],
            h_n[:, SUBLANES - 1], s_n)


def _run(x, pos0, pool, conv, lru, hg, layers, final_norm):
    bsz, t, d = x.shape
    depth = len(layers)
    pools, convs, lrus, hgs = [], [], [], []
    for l, p in enumerate(layers):
        x = _ffn(x.reshape(bsz * t, d), p['ffn1_norm'], p['ffn1_w_gate'], p['ffn1_w_up'],
                 p['ffn1_w_down'], l).reshape(bsz, t, d)
        x, sp, sc, sl, sh = _mixer(x, pos0, pool[l], conv[l], lru[l], hg[l], l, p)
        x = _ffn(x.reshape(bsz * t, d), p['ffn2_norm'], p['ffn2_w_gate'], p['ffn2_w_up'],
                 p['ffn2_w_down'], l, final_norm[None] if l == depth - 1 else None).reshape(bsz, t, d)
        pools.append(sp)
        convs.append(sc)
        lrus.append(sl)
        hgs.append(sh)
    return x, jnp.stack(pools), jnp.stack(convs), jnp.stack(lrus), jnp.stack(hgs)


def kernel(x_prompt, x_sample, state_pool, state_conv, state_lru, state_hgrn, ffn1_norm, ffn1_w_gate, ffn1_w_up, ffn1_w_down, mix_norm, w_in, pool_w, pool_scale, conv_w, conv_b, lru_w_a, lru_b_a, lru_w_x, lru_b_x, lru_lambda, hgrn_lb_logits, hgrn_norm, w_out, ffn2_norm, ffn2_w_gate, ffn2_w_up, ffn2_w_down, final_norm):
    depth = w_in.shape[0]
    big = {'ffn1_w_gate': _to_bf16(ffn1_w_gate), 'ffn1_w_up': _to_bf16(ffn1_w_up),
           'ffn1_w_down': _to_bf16(ffn1_w_down), 'w_in': _to_bf16(w_in, N_FRONT), 'w_out': _to_bf16(w_out),
           'ffn2_w_gate': _to_bf16(ffn2_w_gate), 'ffn2_w_up': _to_bf16(ffn2_w_up),
           'ffn2_w_down': _to_bf16(ffn2_w_down)}
    layers = []
    for l in range(depth):
        layers.append({
            **big, 'ffn1_norm': ffn1_norm[l][None], 'mix_norm': mix_norm[l][None],
            'pool_wbd': _block_diag(pool_w[l]).astype(BF16), 'pool_scale': pool_scale[l][None],
            'conv_w': conv_w[l], 'conv_b': conv_b[l][None],
            'lru_wbd': jnp.concatenate([_block_diag(lru_w_a[l]), _block_diag(lru_w_x[l])],
                                       axis=1).astype(BF16),
            'lru_b': jnp.concatenate([lru_b_a[l], lru_b_x[l]])[None],
            'lru_lambda': lru_lambda[l][None], 'lb_logits': hgrn_lb_logits,
            'hgrn_norm': hgrn_norm[l][None], 'ffn2_norm': ffn2_norm[l][None],
        })
    bp = x_prompt.shape[0]
    dt = x_prompt.dtype
    zero_pool = jnp.zeros((depth, bp) + state_pool.shape[2:], dt)
    zero_conv = jnp.zeros((depth, bp) + state_conv.shape[2:], dt)
    zero_lru = jnp.zeros((depth, bp) + state_lru.shape[2:], dt)
    zero_hgrn = jnp.zeros((depth, bp) + state_hgrn.shape[2:], dt)
    y_p, pool_p, conv_p, lru_p, hgrn_p = _run(x_prompt, 0, zero_pool, zero_conv, zero_lru, zero_hgrn,
                                              layers, final_norm)
    y_s, pool_s, conv_s, lru_s, hgrn_s = _run(x_sample, PAST_LEN, state_pool, state_conv, state_lru,
                                              state_hgrn, layers, final_norm)
    return (y_p, y_s, pool_p, conv_p, lru_p, hgrn_p, pool_s, conv_s, lru_s, hgrn_s)
```

```python
import functools
import math

import jax
import jax.numpy as jnp
from jax import lax
from jax.experimental import pallas as pl
from jax.experimental.pallas import tpu as pltpu

F32 = jnp.float32
BF16 = jnp.bfloat16

EPS = 1e-6
F_MIN = 1e-30
LRU_C = 8.0
LOG2_E = math.log2(math.e)
POOL_WINDOWS = (2, 4, 8, 16)
POOL_GROUP_DIM = 64
POOL_DIM = 256
POOL_HIST = 15
POOL_ROWS = 16
LRU_DIM = 256
CONV_WIDTH = 4
CONV_ROWS = 8
HG_HEADS = 4
HG_DIM = 128
HG_WIDTH = HG_HEADS * HG_DIM
HG_CHUNK = 256
HG_BLOCK = 128
SUBLANES = 8
N_FRONT = POOL_DIM + 2 * LRU_DIM
Z_Q, Z_F, Z_V, Z_G = (slice(i * HG_WIDTH, (i + 1) * HG_WIDTH) for i in range(4))
Z_POOL = slice(4 * HG_WIDTH, 4 * HG_WIDTH + POOL_DIM)
Z_LX = slice(Z_POOL.stop, Z_POOL.stop + LRU_DIM)
Z_LG = slice(Z_LX.stop, Z_LX.stop + LRU_DIM)
MIX_DIM = 1024
PAST_LEN = 2048

VMEM_LIMIT_BYTES = 56 * 1024 * 1024
FFN_ROWS = 1024
CAST_SPLIT = 4
CAST_COLS = 256
MIX_ROWS = 1024


def _rms(x, g):
    ms = jnp.mean(x * x, axis=-1, keepdims=True)
    return x * lax.rsqrt(ms + EPS) * g


def _sigmoid(x):
    return 0.5 * jnp.tanh(0.5 * x) + 0.5


def _silu(x):
    h = 0.5 * x
    return h * jnp.tanh(h) + h


def _dot(a, b):
    return jnp.dot(a, b, preferred_element_type=F32)


def _dot_nt(a, b):
    return lax.dot_general(a, b, (((1,), (1,)), ((), ())), preferred_element_type=F32)


def _dot_tn(a, b):
    return lax.dot_general(a, b, (((0,), (0,)), ((), ())), preferred_element_type=F32)


def _shift_rows(x, d):
    return pltpu.roll(x, d, axis=0)


def _tile_view(x):
    return x.reshape(x.shape[0] // SUBLANES, SUBLANES, x.shape[1])


def _row_of_tile(x3, j):
    return jnp.broadcast_to(x3[:, j:j + 1, :], x3.shape)


def _const_spec(shape, single=False):
    index_map = lambda *_: (0,) * len(shape)
    if single:
        return pl.BlockSpec(shape, index_map, pipeline_mode=pl.Buffered(1))
    return pl.BlockSpec(shape, index_map)


def _layer_spec(shape, layer):
    return pl.BlockSpec((None,) + shape, lambda *_: (layer,) + (0,) * len(shape),
                        pipeline_mode=pl.Buffered(1))


def _cast_kernel(w_ref, o_ref):
    o_ref[...] = w_ref[...].astype(o_ref.dtype)


def _to_bf16(w, front_cols=0):
    depth, r, c = w.shape
    if front_cols:
        width = CAST_COLS
        assert c % width == 0 and front_cols % width == 0
        nb, shift = c // width, front_cols // width
        block = (None, r, width)
        grid = (depth, nb)
        src = lambda l, j: (l, 0, jnp.where(j < nb - shift, j + shift, j - (nb - shift)))
        dst = lambda l, j: (l, 0, j)
    else:
        rows = r // CAST_SPLIT
        assert r % CAST_SPLIT == 0 and rows % 16 == 0
        block = (None, rows, c)
        grid = (depth, CAST_SPLIT)
        src = dst = lambda l, i: (l, i, 0)
    return pl.pallas_call(
        _cast_kernel,
        grid=grid,
        in_specs=[pl.BlockSpec(block, src)],
        out_specs=pl.BlockSpec(block, dst),
        out_shape=jax.ShapeDtypeStruct(w.shape, BF16),
        compiler_params=_params(("parallel", "parallel")),
        name="to_bf16",
    )(w)


def _params(semantics):
    return pltpu.CompilerParams(dimension_semantics=semantics, vmem_limit_bytes=VMEM_LIMIT_BYTES)


def _ffn_kernel(*refs, final):
    if final:
        x_ref, g_ref, wg_ref, wu_ref, wd_ref, fg_ref, o_ref = refs
    else:
        x_ref, g_ref, wg_ref, wu_ref, wd_ref, o_ref = refs
    x = x_ref[...]
    h = _rms(x, g_ref[...]).astype(BF16)
    gate = _dot(h, wg_ref[...])
    up = _dot(h, wu_ref[...])
    act = (gate * jax.nn.sigmoid(gate) * up).astype(BF16)
    out = x + 0.5 * _dot(act, wd_ref[...])
    if final:
        out = _rms(out, fg_ref[...])
    o_ref[...] = out


def _ffn(x2d, norm_g, wg, wu, wd, layer, final_g=None):
    n, d = x2d.shape
    dff = wg.shape[2]
    tm = min(FFN_ROWS, n)
    assert n % tm == 0
    final = final_g is not None
    rows = pl.BlockSpec((tm, d), lambda i: (i, 0))
    in_specs = [rows, _const_spec((1, d)),
                _layer_spec((d, dff), layer), _layer_spec((d, dff), layer), _layer_spec((dff, d), layer)]
    args = [x2d, norm_g, wg, wu, wd]
    if final:
        in_specs.append(_const_spec((1, d)))
        args.append(final_g)
    return pl.pallas_call(
        functools.partial(_ffn_kernel, final=final),
        grid=(n // tm,),
        in_specs=in_specs,
        out_specs=rows,
        out_shape=jax.ShapeDtypeStruct((n, d), F32),
        compiler_params=_params(("parallel",)),
        name="ffn_final" if final else "ffn",
    )(*args)


def _pool_branch(u, hist, pos, wp, scale):
    t = u.shape[0]
    ext = jnp.concatenate([hist, u], axis=0)
    w2 = ext + _shift_rows(ext, 1)
    w4 = w2 + _shift_rows(w2, 2)
    w8 = w4 + _shift_rows(w4, 4)
    w16 = w8 + _shift_rows(w8, 8)
    lane = lax.broadcasted_iota(jnp.int32, (1, POOL_DIM), 1)
    group = lane // POOL_GROUP_DIM
    sums = jnp.where(group == 0, w2, jnp.where(group == 1, w4, jnp.where(group == 2, w8, w16)))
    sums = sums[POOL_ROWS:]
    win = jnp.where(group == 0, POOL_WINDOWS[0],
                    jnp.where(group == 1, POOL_WINDOWS[1],
                              jnp.where(group == 2, POOL_WINDOWS[2], POOL_WINDOWS[3])))
    head = min(t, POOL_ROWS)
    cnt = jnp.minimum(win, pos[:head] + 1).astype(F32)
    means = sums[:head] / cnt
    if t > head:
        means = jnp.concatenate([means, sums[head:] * (1.0 / win.astype(F32))], axis=0)
    pooled = means - u
    ya = _dot(pooled.astype(BF16), wp) * scale
    return ya, ext[t:]


def _lru_branch(xb, gb, hist, h_prev, pos, conv_w, conv_b, wl, bl, lam):
    t = xb.shape[0]
    ext = jnp.concatenate([hist, xb], axis=0)
    conv = jnp.broadcast_to(conv_b, (t, LRU_DIM))
    for k in range(CONV_WIDTH):
        shifted = _shift_rows(ext, CONV_WIDTH - 1 - k) if k < CONV_WIDTH - 1 else ext
        conv = conv + shifted[CONV_ROWS:] * conv_w[k:k + 1]
    gates = _dot(conv.astype(BF16), wl) + bl
    r = _sigmoid(gates[:, :LRU_DIM])
    ig = _sigmoid(gates[:, LRU_DIM:])
    a = jnp.exp2((-LRU_C * LOG2_E * jax.nn.softplus(-lam)) * r)
    gap = jnp.maximum(1.0 - a * a, 0.0)
    mult = jnp.where(gap > 0.0, gap * lax.rsqrt(gap), 0.0)
    first = jnp.where(pos[:SUBLANES] == 0, 1.0, mult[:SUBLANES])
    mult = jnp.concatenate([first, mult[SUBLANES:]], axis=0) if t > SUBLANES else first
    bterm = mult * (ig * conv)
    acc_a, acc_b = _tile_view(a), _tile_view(bterm)
    sub = lax.broadcasted_iota(jnp.int32, (1, SUBLANES, 1), 1)
    d = 1
    while d < SUBLANES:
        keep = sub >= d
        prev_a = jnp.where(keep, pltpu.roll(acc_a, d, axis=1), 1.0)
        prev_b = jnp.where(keep, pltpu.roll(acc_b, d, axis=1), 0.0)
        acc_b = acc_a * prev_b + acc_b
        acc_a = acc_a * prev_a
        d *= 2
    tiles = []
    carry = h_prev
    for i in range(t // SUBLANES):
        h_i = acc_a[i] * carry + acc_b[i]
        tiles.append(h_i)
        carry = h_i[SUBLANES - 1:SUBLANES]
    h = jnp.concatenate(tiles, axis=0)
    yb = h * jax.nn.gelu(gb)
    return yb, ext[t:], tiles[-1]


def _hgrn_chunk(q, fz, v, g, lb, norm_g, st_ref):
    c = q.shape[0]
    nt = c // SUBLANES
    one_m_lb = 1.0 - lb
    f = lb + one_m_lb * _sigmoid(fz)
    fc = jnp.maximum(f, F_MIN)
    log_f = jnp.log2(fc)
    kk = 1.0 - f
    qf = _silu(q)

    sub = lax.broadcasted_iota(jnp.int32, (1, SUBLANES, 1), 1)
    c3 = _tile_view(log_f)
    d = 1
    while d < SUBLANES:
        c3 = c3 + jnp.where(sub >= d, pltpu.roll(c3, d, axis=1), 0.0)
        d *= 2
    carries = [jnp.zeros((1, 1, HG_WIDTH), F32)]
    for i in range(1, nt):
        carries.append(carries[-1] + c3[i - 1:i, SUBLANES - 1:SUBLANES, :])
    b = (c3 + jnp.concatenate(carries, axis=0)).reshape(c, HG_WIDTH)
    b_last = b[c - 1:c]
    q_in = (qf * jnp.exp2(b)).astype(BF16)
    k_out = (kk * jnp.exp2(b_last - b)).astype(BF16)
    decay = jnp.exp2(b_last)
    v16 = v.astype(BF16)

    blk = min(HG_BLOCK, c)
    wide = []
    hs = c // 2
    while hs >= blk:
        for m in range(c // (2 * hs)):
            lo, mid, hi = m * 2 * hs, m * 2 * hs + hs, (m + 1) * 2 * hs
            ref = b[mid - 1:mid]
            wide.append((lo, mid, hi, (qf[mid:hi] * jnp.exp2(b[mid:hi] - ref)).astype(BF16),
                         (kk[lo:mid] * jnp.exp2(ref - b[lo:mid])).astype(BF16)))
        hs //= 2
    levels = []
    while hs >= SUBLANES:
        q_parts, k_parts = [], []
        zero = jnp.zeros((hs, HG_WIDTH), BF16)
        for m in range(c // (2 * hs)):
            lo, mid, hi = m * 2 * hs, m * 2 * hs + hs, (m + 1) * 2 * hs
            ref = b[mid - 1:mid]
            k_parts += [(kk[lo:mid] * jnp.exp2(ref - b[lo:mid])).astype(BF16), zero]
            q_parts += [zero, (qf[mid:hi] * jnp.exp2(b[mid:hi] - ref)).astype(BF16)]
        levels.append((hs, jnp.concatenate(q_parts, axis=0), jnp.concatenate(k_parts, axis=0)))
        hs //= 2
    qf3, kk3 = _tile_view(qf), _tile_view(kk)
    while hs >= 2:
        if 2 * hs == SUBLANES:
            ref = _row_of_tile(c3, hs - 1)
        else:
            ref = jnp.where(sub < 2 * hs, _row_of_tile(c3, hs - 1), _row_of_tile(c3, 3 * hs - 1))
        upper = (sub & hs) != 0
        scaled = jnp.where(upper, qf3, kk3) * jnp.exp2(-jnp.abs(c3 - ref))
        q_l = jnp.where(upper, scaled, 0.0).astype(BF16).reshape(c, HG_WIDTH)
        k_l = jnp.where(upper, 0.0, scaled).astype(BF16).reshape(c, HG_WIDTH)
        levels.append((hs, q_l, k_l))
        hs //= 2
    odd = (sub & 1) != 0
    q_l = jnp.where(odd, qf3 * _tile_view(fc), 0.0).astype(BF16).reshape(c, HG_WIDTH)
    k_l = jnp.where(odd, 0.0, kk3).astype(BF16).reshape(c, HG_WIDTH)
    levels.append((1, q_l, k_l))
    qk = qf * kk

    ti = lax.broadcasted_iota(jnp.int32, (blk, blk), 0)
    si = lax.broadcasted_iota(jnp.int32, (blk, blk), 1)
    blk_xor = ti ^ si

    outs = []
    for h in range(HG_HEADS):
        sl = slice(h * HG_DIM, (h + 1) * HG_DIM)
        st = st_ref[h]
        nb = c // blk
        tiles = [[None] * nb for _ in range(nb)]
        for d in range(nb):
            rows = slice(d * blk, (d + 1) * blk)
            att = jnp.zeros((blk, blk), F32)
            for hs, q_l, k_l in reversed(levels):
                att = jnp.where(blk_xor >= hs, _dot_nt(q_l[rows, sl], k_l[rows, sl]), att)
            tiles[d][d] = att.astype(BF16)
        for lo, mid, hi, q_w, k_w in wide:
            a_w = _dot_nt(q_w[:, sl], k_w[:, sl]).astype(BF16)
            for r in range(mid, hi, blk):
                for s0 in range(lo, mid, blk):
                    tiles[r // blk][s0 // blk] = a_w[r - mid:r - mid + blk, s0 - lo:s0 - lo + blk]
        zero_tile = jnp.zeros((blk, blk), BF16)
        att = jnp.concatenate(
            [jnp.concatenate([t if t is not None else zero_tile for t in row], axis=1) for row in tiles],
            axis=0)
        o = _dot_nt(q_in[:, sl], st.astype(BF16)) + _dot(att, v16[:, sl])
        o = o + jnp.sum(qk[:, sl], axis=-1, keepdims=True) * v[:, sl]
        st_ref[h] = decay[:, sl] * st + _dot_tn(v16[:, sl], k_out[:, sl])
        o = o * lax.rsqrt(jnp.mean(o * o, axis=-1, keepdims=True) + EPS)
        outs.append(o)
    o = jnp.concatenate(outs, axis=-1) * norm_g
    return o * _silu(g)


def _lower_bound(lbl_ref, layer):
    logits = lbl_ref[...]
    ex = jnp.exp(logits - jnp.max(logits, axis=0, keepdims=True))
    p = ex / jnp.sum(ex, axis=0, keepdims=True)
    return jnp.maximum(jnp.sum(p[0:layer + 1], axis=0, keepdims=True) - p[0:1], 0.0)


def _mixer_kernel(x_ref, ng_ref, win_ref, wp_ref, ps_ref, cw_ref, cb_ref, wl_ref, bl_ref, lam_ref,
                  lbl_ref, hn_ref, wout_ref, pool0_ref, conv0_ref, h0_ref, s0_ref,
                  y_ref, pool_o_ref, conv_o_ref, h_o_ref, s_o_ref,
                  z_scr, mix_scr, pool_c, conv_c, h_c, st_c, *, layer, pos0, chunk):
    i = pl.program_id(1)
    tt = x_ref.shape[0]

    @pl.when(i == 0)
    def _():
        pool_c[...] = pool0_ref[...]
        conv_c[...] = conv0_ref[...]
        h_c[...] = h0_ref[...]
        for h in range(HG_HEADS):
            st_c[h] = s0_ref[h].T

    x = x_ref[...]
    hn = _rms(x, ng_ref[...]).astype(BF16)
    pos = pos0 + i * tt + lax.broadcasted_iota(jnp.int32, (tt, 1), 0)
    lb = _lower_bound(lbl_ref, layer)
    hnorm = hn_ref[...]

    def hgrn(r0):
        rows = slice(r0, r0 + chunk)
        yc = _hgrn_chunk(z_scr[rows, Z_Q], z_scr[rows, Z_F], z_scr[rows, Z_V], z_scr[rows, Z_G],
                         lb, hnorm, st_c)
        mix_scr[rows, POOL_DIM + LRU_DIM:MIX_DIM] = yc.astype(BF16)

    z_scr[:, :Z_POOL.start] = _dot(hn, win_ref[:, :Z_POOL.start])
    hgrn(0)
    z_scr[:, Z_POOL.start:] = _dot(hn, win_ref[:, Z_POOL.start:])

    ya, new_pool = _pool_branch(z_scr[:, Z_POOL], pool_c[...], pos, wp_ref[...], ps_ref[...])
    pool_c[...] = new_pool
    mix_scr[:, 0:POOL_DIM] = ya.astype(BF16)

    yb, new_conv, new_h = _lru_branch(
        z_scr[:, Z_LX], z_scr[:, Z_LG], conv_c[...], h_c[SUBLANES - 1:SUBLANES],
        pos, cw_ref[...], cb_ref[...], wl_ref[...], bl_ref[...], lam_ref[...])
    conv_c[...] = new_conv
    h_c[...] = new_h
    mix_scr[:, POOL_DIM:POOL_DIM + LRU_DIM] = yb.astype(BF16)

    for r0 in range(chunk, tt, chunk):
        hgrn(r0)
    y_ref[...] = x + _dot(mix_scr[...], wout_ref[...])

    @pl.when(i == pl.num_programs(1) - 1)
    def _():
        pool_o_ref[...] = pool_c[...]
        conv_o_ref[...] = conv_c[...]
        h_o_ref[...] = h_c[...]
        for h in range(HG_HEADS):
            s_o_ref[h] = st_c[h].T


class _StateIO:
    def __init__(self, src, dst, seq):
        self.src, self.dst, self.seq = src, dst, seq

    def __getitem__(self, h):
        return self.src[self.seq, h].T

    def __setitem__(self, h, val):
        self.dst[self.seq, h] = val.T


def _mixer_group_kernel(x_ref, ng_ref, win_ref, wp_ref, ps_ref, cw_ref, cb_ref, wl_ref, bl_ref, lam_ref,
                        lbl_ref, hn_ref, wout_ref, pool0_ref, conv0_ref, h0_ref, s0_ref,
                        y_ref, pool_o_ref, conv_o_ref, h_o_ref, s_o_ref, z_scr, mix_scr, *, layer, pos0):
    g, t, d = x_ref.shape
    x = x_ref[...].reshape(g * t, d)
    hn = _rms(x, ng_ref[...]).astype(BF16)
    z_scr[...] = _dot(hn, win_ref[...])
    pos = pos0 + lax.broadcasted_iota(jnp.int32, (t, 1), 0)
    lb = _lower_bound(lbl_ref, layer)
    hnorm = hn_ref[...]
    for s in range(g):
        rows = slice(s * t, (s + 1) * t)
        ya, new_pool = _pool_branch(z_scr[rows, Z_POOL], pool0_ref[s], pos, wp_ref[...], ps_ref[...])
        pool_o_ref[s] = new_pool
        mix_scr[rows, 0:POOL_DIM] = ya.astype(BF16)
        yb, new_conv, new_h = _lru_branch(
            z_scr[rows, Z_LX], z_scr[rows, Z_LG], conv0_ref[s], h0_ref[s, SUBLANES - 1:SUBLANES],
            pos, cw_ref[...], cb_ref[...], wl_ref[...], bl_ref[...], lam_ref[...])
        conv_o_ref[s] = new_conv
        h_o_ref[s] = new_h
        mix_scr[rows, POOL_DIM:POOL_DIM + LRU_DIM] = yb.astype(BF16)
        yc = _hgrn_chunk(z_scr[rows, Z_Q], z_scr[rows, Z_F], z_scr[rows, Z_V], z_scr[rows, Z_G],
                         lb, hnorm, _StateIO(s0_ref, s_o_ref, s))
        mix_scr[rows, POOL_DIM + LRU_DIM:MIX_DIM] = yc.astype(BF16)
    y_ref[...] = (x + _dot(mix_scr[...], wout_ref[...])).reshape(g, t, d)


def _block_diag(w):
    g, c, d = w.shape
    eye = jnp.eye(g, dtype=w.dtype)
    return (eye[:, None, :, None] * w[:, :, None, :]).reshape(g * c, g * d)


STATE_SHAPES = ((POOL_ROWS, POOL_DIM), (CONV_ROWS, LRU_DIM), (SUBLANES, LRU_DIM), (HG_HEADS, HG_DIM, HG_DIM))


def _mixer(x, pos0, pool0, conv0, h0, s0, layer, p):
    bsz, t, d = x.shape
    tt = min(MIX_ROWS, t)
    assert t % tt == 0 and tt % POOL_ROWS == 0
    chunk = min(HG_CHUNK, tt)
    assert tt % chunk == 0
    in_dim = p['w_in'].shape[2]
    depth = p['lb_logits'].shape[0]

    group = min(bsz, MIX_ROWS // t) if t <= HG_CHUNK else 1
    states = [jnp.pad(pool0, ((0, 0), (POOL_ROWS - pool0.shape[1], 0), (0, 0))),
              jnp.pad(conv0, ((0, 0), (CONV_ROWS - conv0.shape[1], 0), (0, 0))),
              jnp.pad(h0[:, None, :], ((0, 0), (SUBLANES - 1, 0), (0, 0))), s0]
    if group > 1:
        assert bsz % group == 0
        kern = functools.partial(_mixer_group_kernel, layer=layer, pos0=pos0)
        state_specs = [pl.BlockSpec((group,) + sh, lambda b, i, sh=sh: (b,) + (0,) * len(sh))
                       for sh in STATE_SHAPES]
        tile = pl.BlockSpec((group, t, d), lambda b, i: (b, 0, 0))
        grid = (bsz // group, 1)
        scratch = [pltpu.VMEM((group * t, in_dim), F32), pltpu.VMEM((group * t, MIX_DIM), BF16)]
    else:
        kern = functools.partial(_mixer_kernel, layer=layer, pos0=pos0, chunk=chunk)
        state_specs = [pl.BlockSpec((None,) + sh, lambda b, i, sh=sh: (b,) + (0,) * len(sh))
                       for sh in STATE_SHAPES]
        tile = pl.BlockSpec((None, tt, d), lambda b, i: (b, i, 0))
        grid = (bsz, t // tt)
        scratch = ([pltpu.VMEM((tt, in_dim), F32), pltpu.VMEM((tt, MIX_DIM), BF16)]
                   + [pltpu.VMEM(sh, F32) for sh in STATE_SHAPES])
    in_specs = [tile, _const_spec((1, d)), _layer_spec((d, in_dim), layer),
                _const_spec((POOL_DIM, POOL_DIM)), _const_spec((1, POOL_DIM)),
                _const_spec((CONV_WIDTH, LRU_DIM)), _const_spec((1, LRU_DIM)),
                _const_spec((LRU_DIM, 2 * LRU_DIM)), _const_spec((1, 2 * LRU_DIM)), _const_spec((1, LRU_DIM)),
                _const_spec((depth, HG_WIDTH)), _const_spec((1, HG_WIDTH)),
                _layer_spec((MIX_DIM, d), layer)] + state_specs
    y, pool_n, conv_n, h_n, s_n = pl.pallas_call(
        kern,
        grid=grid,
        in_specs=in_specs,
        out_specs=[tile] + state_specs,
        out_shape=[jax.ShapeDtypeStruct((bsz, t, d), F32)]
                  + [jax.ShapeDtypeStruct((bsz,) + sh, F32) for sh in STATE_SHAPES],
        scratch_shapes=scratch,
        compiler_params=_params(("parallel", "arbitrary")),
        name="mixer",
    )(x, p['mix_norm'], p['w_in'], p['pool_wbd'], p['pool_scale'], p['conv_w'], p['conv_b'],
      p['lru_wbd'], p['lru_b'], p['lru_lambda'], p['lb_logits'], p['hgrn_norm'], p['w_out'], *states)
    return (y, pool_n[:, POOL_ROWS - POOL_HIST:], conv_n[:, CONV_ROWS - (CONV_WIDTH - 1):],
            h_n[:, SUBLANES - 1], s_n)


def _run(x, pos0, pool, conv, lru, hg, layers, final_norm):
    bsz, t, d = x.shape
    depth = len(layers)
    pools, convs, lrus, hgs = [], [], [], []
    for l, p in enumerate(layers):
        x = _ffn(x.reshape(bsz * t, d), p['ffn1_norm'], p['ffn1_w_gate'], p['ffn1_w_up'],
                 p['ffn1_w_down'], l).reshape(bsz, t, d)
        x, sp, sc, sl, sh = _mixer(x, pos0, pool[l], conv[l], lru[l], hg[l], l, p)
        x = _ffn(x.reshape(bsz * t, d), p['ffn2_norm'], p['ffn2_w_gate'], p['ffn2_w_up'],
                 p['ffn2_w_down'], l, final_norm[None] if l == depth - 1 else None).reshape(bsz, t, d)
        pools.append(sp)
        convs.append(sc)
        lrus.append(sl)
        hgs.append(sh)
    return x, jnp.stack(pools), jnp.stack(convs), jnp.stack(lrus), jnp.stack(hgs)


def kernel(x_prompt, x_sample, state_pool, state_conv, state_lru, state_hgrn, ffn1_norm, ffn1_w_gate, ffn1_w_up, ffn1_w_down, mix_norm, w_in, pool_w, pool_scale, conv_w, conv_b, lru_w_a, lru_b_a, lru_w_x, lru_b_x, lru_lambda, hgrn_lb_logits, hgrn_norm, w_out, ffn2_norm, ffn2_w_gate, ffn2_w_up, ffn2_w_down, final_norm):
    depth = w_in.shape[0]
    big = {'ffn1_w_gate': _to_bf16(ffn1_w_gate), 'ffn1_w_up': _to_bf16(ffn1_w_up),
           'ffn1_w_down': _to_bf16(ffn1_w_down), 'w_in': _to_bf16(w_in, N_FRONT), 'w_out': _to_bf16(w_out),
           'ffn2_w_gate': _to_bf16(ffn2_w_gate), 'ffn2_w_up': _to_bf16(ffn2_w_up),
           'ffn2_w_down': _to_bf16(ffn2_w_down)}
    layers = []
    for l in range(depth):
        layers.append({
            **big, 'ffn1_norm': ffn1_norm[l][None], 'mix_norm': mix_norm[l][None],
            'pool_wbd': _block_diag(pool_w[l]).astype(BF16), 'pool_scale': pool_scale[l][None],
            'conv_w': conv_w[l], 'conv_b': conv_b[l][None],
            'lru_wbd': jnp.concatenate([_block_diag(lru_w_a[l]), _block_diag(lru_w_x[l])],
                                       axis=1).astype(BF16),
            'lru_b': jnp.concatenate([lru_b_a[l], lru_b_x[l]])[None],
            'lru_lambda': lru_lambda[l][None], 'lb_logits': hgrn_lb_logits,
            'hgrn_norm': hgrn_norm[l][None], 'ffn2_norm': ffn2_norm[l][None],
        })
    bp = x_prompt.shape[0]
    dt = x_prompt.dtype
    zero_pool = jnp.zeros((depth, bp) + state_pool.shape[2:], dt)
    zero_conv = jnp.zeros((depth, bp) + state_conv.shape[2:], dt)
    zero_lru = jnp.zeros((depth, bp) + state_lru.shape[2:], dt)
    zero_hgrn = jnp.zeros((depth, bp) + state_hgrn.shape[2:], dt)
    y_p, pool_p, conv_p, lru_p, hgrn_p = _run(x_prompt, 0, zero_pool, zero_conv, zero_lru, zero_hgrn,
                                              layers, final_norm)
    y_s, pool_s, conv_s, lru_s, hgrn_s = _run(x_sample, PAST_LEN, state_pool, state_conv, state_lru,
                                              state_hgrn, layers, final_norm)
    return (y_p, y_s, pool_p, conv_p, lru_p, hgrn_p, pool_s, conv_s, lru_s, hgrn_s)
```

```python
import functools
import math

import jax
import jax.numpy as jnp
from jax import lax
from jax.experimental import pallas as pl
from jax.experimental.pallas import tpu as pltpu

F32 = jnp.float32
BF16 = jnp.bfloat16

EPS = 1e-6
F_MIN = 1e-30
LRU_C = 8.0
LOG2_E = math.log2(math.e)
POOL_WINDOWS = (2, 4, 8, 16)
POOL_GROUP_DIM = 64
POOL_DIM = 256
POOL_HIST = 15
POOL_ROWS = 16
LRU_DIM = 256
CONV_WIDTH = 4
CONV_ROWS = 8
HG_HEADS = 4
HG_DIM = 128
HG_WIDTH = HG_HEADS * HG_DIM
HG_CHUNK = 256
HG_BLOCK = 128
SUBLANES = 8
N_FRONT = POOL_DIM + 2 * LRU_DIM
Z_Q, Z_F, Z_V, Z_G = (slice(i * HG_WIDTH, (i + 1) * HG_WIDTH) for i in range(4))
Z_POOL = slice(4 * HG_WIDTH, 4 * HG_WIDTH + POOL_DIM)
Z_LX = slice(Z_POOL.stop, Z_POOL.stop + LRU_DIM)
Z_LG = slice(Z_LX.stop, Z_LX.stop + LRU_DIM)
MIX_DIM = 1024
PAST_LEN = 2048

VMEM_LIMIT_BYTES = 56 * 1024 * 1024
FFN_ROWS = 1024
CAST_SPLIT = 4
CAST_COLS = 256
MIX_ROWS = 1024


def _rms(x, g):
    ms = jnp.mean(x * x, axis=-1, keepdims=True)
    return x * lax.rsqrt(ms + EPS) * g


def _sigmoid(x):
    return 0.5 * jnp.tanh(0.5 * x) + 0.5


def _silu(x):
    h = 0.5 * x
    return h * jnp.tanh(h) + h


def _dot(a, b):
    return jnp.dot(a, b, preferred_element_type=F32)


def _dot_nt(a, b):
    return lax.dot_general(a, b, (((1,), (1,)), ((), ())), preferred_element_type=F32)


def _dot_tn(a, b):
    return lax.dot_general(a, b, (((0,), (0,)), ((), ())), preferred_element_type=F32)


def _shift_rows(x, d):
    return pltpu.roll(x, d, axis=0)


def _tile_view(x):
    return x.reshape(x.shape[0] // SUBLANES, SUBLANES, x.shape[1])


def _row_of_tile(x3, j):
    return jnp.broadcast_to(x3[:, j:j + 1, :], x3.shape)


def _const_spec(shape, single=False):
    index_map = lambda *_: (0,) * len(shape)
    if single:
        return pl.BlockSpec(shape, index_map, pipeline_mode=pl.Buffered(1))
    return pl.BlockSpec(shape, index_map)


def _layer_spec(shape, layer):
    return pl.BlockSpec((None,) + shape, lambda *_: (layer,) + (0,) * len(shape),
                        pipeline_mode=pl.Buffered(1))


def _cast_kernel(w_ref, o_ref):
    o_ref[...] = w_ref[...].astype(o_ref.dtype)


def _to_bf16(w, front_cols=0):
    depth, r, c = w.shape
    if front_cols:
        width = CAST_COLS
        assert c % width == 0 and front_cols % width == 0
        nb, shift = c // width, front_cols // width
        block = (None, r, width)
        grid = (depth, nb)
        src = lambda l, j: (l, 0, jnp.where(j < nb - shift, j + shift, j - (nb - shift)))
        dst = lambda l, j: (l, 0, j)
    else:
        rows = r // CAST_SPLIT
        assert r % CAST_SPLIT == 0 and rows % 16 == 0
        block = (None, rows, c)
        grid = (depth, CAST_SPLIT)
        src = dst = lambda l, i: (l, i, 0)
    return pl.pallas_call(
        _cast_kernel,
        grid=grid,
        in_specs=[pl.BlockSpec(block, src)],
        out_specs=pl.BlockSpec(block, dst),
        out_shape=jax.ShapeDtypeStruct(w.shape, BF16),
        compiler_params=_params(("parallel", "parallel")),
        name="to_bf16",
    )(w)


def _params(semantics):
    return pltpu.CompilerParams(dimension_semantics=semantics, vmem_limit_bytes=VMEM_LIMIT_BYTES)


def _ffn_kernel(*refs, final, n_main):
    if final:
        xp_ref, xs_ref, g_ref, wg_ref, wu_ref, wd_ref, fg_ref, op_ref, os_ref = refs
    else:
        xp_ref, xs_ref, g_ref, wg_ref, wu_ref, wd_ref, op_ref, os_ref = refs

    def body(x_ref, o_ref):
        x = x_ref[...]
        h = _rms(x, g_ref[...]).astype(BF16)
        gate = _dot(h, wg_ref[...])
        up = _dot(h, wu_ref[...])
        act = (gate * jax.nn.sigmoid(gate) * up).astype(BF16)
        out = x + 0.5 * _dot(act, wd_ref[...])
        if final:
            out = _rms(out, fg_ref[...])
        o_ref[...] = out

    i = pl.program_id(0)

    @pl.when(i < n_main)
    def _():
        body(xp_ref, op_ref)

    @pl.when(i == n_main)
    def _():
        body(xs_ref, os_ref)


def _ffn(xp, xs, norm_g, wg, wu, wd, layer, final_g=None):
    n, d = xp.shape
    ns = xs.shape[0]
    dff = wg.shape[2]
    tm = min(FFN_ROWS, n)
    assert n % tm == 0 and ns <= tm and ns % SUBLANES == 0
    n_main = n // tm
    final = final_g is not None
    rows = pl.BlockSpec((tm, d), lambda i: (jnp.minimum(i, n_main - 1), 0))
    in_specs = [rows, _const_spec((ns, d), True), _const_spec((1, d)),
                _layer_spec((d, dff), layer), _layer_spec((d, dff), layer), _layer_spec((dff, d), layer)]
    args = [xp, xs, norm_g, wg, wu, wd]
    if final:
        in_specs.append(_const_spec((1, d)))
        args.append(final_g)
    return pl.pallas_call(
        functools.partial(_ffn_kernel, final=final, n_main=n_main),
        grid=(n_main + 1,),
        in_specs=in_specs,
        out_specs=[rows, _const_spec((ns, d))],
        out_shape=[jax.ShapeDtypeStruct((n, d), F32), jax.ShapeDtypeStruct((ns, d), F32)],
        compiler_params=_params(("arbitrary",)),
        name="ffn_final" if final else "ffn",
    )(*args)


def _pool_branch(u, hist, pos, wp, scale):
    t = u.shape[0]
    ext = jnp.concatenate([hist, u], axis=0)
    w2 = ext + _shift_rows(ext, 1)
    w4 = w2 + _shift_rows(w2, 2)
    w8 = w4 + _shift_rows(w4, 4)
    w16 = w8 + _shift_rows(w8, 8)
    lane = lax.broadcasted_iota(jnp.int32, (1, POOL_DIM), 1)
    group = lane // POOL_GROUP_DIM
    sums = jnp.where(group == 0, w2, jnp.where(group == 1, w4, jnp.where(group == 2, w8, w16)))
    sums = sums[POOL_ROWS:]
    win = jnp.where(group == 0, POOL_WINDOWS[0],
                    jnp.where(group == 1, POOL_WINDOWS[1],
                              jnp.where(group == 2, POOL_WINDOWS[2], POOL_WINDOWS[3])))
    head = min(t, POOL_ROWS)
    cnt = jnp.minimum(win, pos[:head] + 1).astype(F32)
    means = sums[:head] / cnt
    if t > head:
        means = jnp.concatenate([means, sums[head:] * (1.0 / win.astype(F32))], axis=0)
    pooled = means - u
    ya = _dot(pooled.astype(BF16), wp) * scale
    return ya, ext[t:]


def _lru_branch(xb, gb, hist, h_prev, pos, conv_w, conv_b, wl, bl, lam):
    t = xb.shape[0]
    ext = jnp.concatenate([hist, xb], axis=0)
    conv = jnp.broadcast_to(conv_b, (t, LRU_DIM))
    for k in range(CONV_WIDTH):
        shifted = _shift_rows(ext, CONV_WIDTH - 1 - k) if k < CONV_WIDTH - 1 else ext
        conv = conv + shifted[CONV_ROWS:] * conv_w[k:k + 1]
    gates = _dot(conv.astype(BF16), wl) + bl
    r = _sigmoid(gates[:, :LRU_DIM])
    ig = _sigmoid(gates[:, LRU_DIM:])
    a = jnp.exp2((-LRU_C * LOG2_E * jax.nn.softplus(-lam)) * r)
    gap = jnp.maximum(1.0 - a * a, 0.0)
    mult = jnp.where(gap > 0.0, gap * lax.rsqrt(gap), 0.0)
    first = jnp.where(pos[:SUBLANES] == 0, 1.0, mult[:SUBLANES])
    mult = jnp.concatenate([first, mult[SUBLANES:]], axis=0) if t > SUBLANES else first
    bterm = mult * (ig * conv)
    acc_a, acc_b = _tile_view(a), _tile_view(bterm)
    sub = lax.broadcasted_iota(jnp.int32, (1, SUBLANES, 1), 1)
    d = 1
    while d < SUBLANES:
        keep = sub >= d
        prev_a = jnp.where(keep, pltpu.roll(acc_a, d, axis=1), 1.0)
        prev_b = jnp.where(keep, pltpu.roll(acc_b, d, axis=1), 0.0)
        acc_b = acc_a * prev_b + acc_b
        acc_a = acc_a * prev_a
        d *= 2
    tiles = []
    carry = h_prev
    for i in range(t // SUBLANES):
        h_i = acc_a[i] * carry + acc_b[i]
        tiles.append(h_i)
        carry = h_i[SUBLANES - 1:SUBLANES]
    h = jnp.concatenate(tiles, axis=0)
    yb = h * jax.nn.gelu(gb)
    return yb, ext[t:], tiles[-1]


def _hgrn_chunk(q, fz, v, g, lb, norm_g, st_ref):
    c = q.shape[0]
    nt = c // SUBLANES
    one_m_lb = 1.0 - lb
    f = lb + one_m_lb * _sigmoid(fz)
    fc = jnp.maximum(f, F_MIN)
    log_f = jnp.log2(fc)
    kk = 1.0 - f
    qf = _silu(q)

    sub = lax.broadcasted_iota(jnp.int32, (1, SUBLANES, 1), 1)
    c3 = _tile_view(log_f)
    d = 1
    while d < SUBLANES:
        c3 = c3 + jnp.where(sub >= d, pltpu.roll(c3, d, axis=1), 0.0)
        d *= 2
    carries = [jnp.zeros((1, 1, HG_WIDTH), F32)]
    for i in range(1, nt):
        carries.append(carries[-1] + c3[i - 1:i, SUBLANES - 1:SUBLANES, :])
    b = (c3 + jnp.concatenate(carries, axis=0)).reshape(c, HG_WIDTH)
    b_last = b[c - 1:c]
    q_in = (qf * jnp.exp2(b)).astype(BF16)
    k_out = (kk * jnp.exp2(b_last - b)).astype(BF16)
    decay = jnp.exp2(b_last)
    v16 = v.astype(BF16)

    blk = min(HG_BLOCK, c)
    wide = []
    hs = c // 2
    while hs >= blk:
        for m in range(c // (2 * hs)):
            lo, mid, hi = m * 2 * hs, m * 2 * hs + hs, (m + 1) * 2 * hs
            ref = b[mid - 1:mid]
            wide.append((lo, mid, hi, (qf[mid:hi] * jnp.exp2(b[mid:hi] - ref)).astype(BF16),
                         (kk[lo:mid] * jnp.exp2(ref - b[lo:mid])).astype(BF16)))
        hs //= 2
    levels = []
    while hs >= SUBLANES:
        q_parts, k_parts = [], []
        zero = jnp.zeros((hs, HG_WIDTH), BF16)
        for m in range(c // (2 * hs)):
            lo, mid, hi = m * 2 * hs, m * 2 * hs + hs, (m + 1) * 2 * hs
            ref = b[mid - 1:mid]
            k_parts += [(kk[lo:mid] * jnp.exp2(ref - b[lo:mid])).astype(BF16), zero]
            q_parts += [zero, (qf[mid:hi] * jnp.exp2(b[mid:hi] - ref)).astype(BF16)]
        levels.append((hs, jnp.concatenate(q_parts, axis=0), jnp.concatenate(k_parts, axis=0)))
        hs //= 2
    qf3, kk3 = _tile_view(qf), _tile_view(kk)
    while hs >= 2:
        if 2 * hs == SUBLANES:
            ref = _row_of_tile(c3, hs - 1)
        else:
            ref = jnp.where(sub < 2 * hs, _row_of_tile(c3, hs - 1), _row_of_tile(c3, 3 * hs - 1))
        upper = (sub & hs) != 0
        scaled = jnp.where(upper, qf3, kk3) * jnp.exp2(-jnp.abs(c3 - ref))
        q_l = jnp.where(upper, scaled, 0.0).astype(BF16).reshape(c, HG_WIDTH)
        k_l = jnp.where(upper, 0.0, scaled).astype(BF16).reshape(c, HG_WIDTH)
        levels.append((hs, q_l, k_l))
        hs //= 2
    odd = (sub & 1) != 0
    q_l = jnp.where(odd, qf3 * _tile_view(fc), 0.0).astype(BF16).reshape(c, HG_WIDTH)
    k_l = jnp.where(odd, 0.0, kk3).astype(BF16).reshape(c, HG_WIDTH)
    levels.append((1, q_l, k_l))
    qk = qf * kk

    ti = lax.broadcasted_iota(jnp.int32, (blk, blk), 0)
    si = lax.broadcasted_iota(jnp.int32, (blk, blk), 1)
    blk_xor = ti ^ si

    outs = []
    for h in range(HG_HEADS):
        sl = slice(h * HG_DIM, (h + 1) * HG_DIM)
        st = st_ref[h]
        nb = c // blk
        tiles = [[None] * nb for _ in range(nb)]
        for d in range(nb):
            rows = slice(d * blk, (d + 1) * blk)
            att = jnp.zeros((blk, blk), F32)
            for hs, q_l, k_l in reversed(levels):
                att = jnp.where(blk_xor >= hs, _dot_nt(q_l[rows, sl], k_l[rows, sl]), att)
            tiles[d][d] = att.astype(BF16)
        for lo, mid, hi, q_w, k_w in wide:
            a_w = _dot_nt(q_w[:, sl], k_w[:, sl]).astype(BF16)
            for r in range(mid, hi, blk):
                for s0 in range(lo, mid, blk):
                    tiles[r // blk][s0 // blk] = a_w[r - mid:r - mid + blk, s0 - lo:s0 - lo + blk]
        zero_tile = jnp.zeros((blk, blk), BF16)
        att = jnp.concatenate(
            [jnp.concatenate([t if t is not None else zero_tile for t in row], axis=1) for row in tiles],
            axis=0)
        o = _dot_nt(q_in[:, sl], st.astype(BF16)) + _dot(att, v16[:, sl])
        o = o + jnp.sum(qk[:, sl], axis=-1, keepdims=True) * v[:, sl]
        st_ref[h] = decay[:, sl] * st + _dot_tn(v16[:, sl], k_out[:, sl])
        o = o * lax.rsqrt(jnp.mean(o * o, axis=-1, keepdims=True) + EPS)
        outs.append(o)
    o = jnp.concatenate(outs, axis=-1) * norm_g
    return o * _silu(g)


def _lower_bound(lbl_ref, layer):
    logits = lbl_ref[...]
    ex = jnp.exp(logits - jnp.max(logits, axis=0, keepdims=True))
    p = ex / jnp.sum(ex, axis=0, keepdims=True)
    return jnp.maximum(jnp.sum(p[0:layer + 1], axis=0, keepdims=True) - p[0:1], 0.0)


def _mixer_kernel(x_ref, ng_ref, win_ref, wp_ref, ps_ref, cw_ref, cb_ref, wl_ref, bl_ref, lam_ref,
                  lbl_ref, hn_ref, wout_ref, pool0_ref, conv0_ref, h0_ref, s0_ref,
                  y_ref, pool_o_ref, conv_o_ref, h_o_ref, s_o_ref,
                  z_scr, mix_scr, pool_c, conv_c, h_c, st_c, *, layer, pos0, chunk):
    i = pl.program_id(1)
    tt = x_ref.shape[0]

    @pl.when(i == 0)
    def _():
        pool_c[...] = pool0_ref[...]
        conv_c[...] = conv0_ref[...]
        h_c[...] = h0_ref[...]
        for h in range(HG_HEADS):
            st_c[h] = s0_ref[h].T

    x = x_ref[...]
    hn = _rms(x, ng_ref[...]).astype(BF16)
    pos = pos0 + i * tt + lax.broadcasted_iota(jnp.int32, (tt, 1), 0)
    lb = _lower_bound(lbl_ref, layer)
    hnorm = hn_ref[...]

    def hgrn(r0):
        rows = slice(r0, r0 + chunk)
        yc = _hgrn_chunk(z_scr[rows, Z_Q], z_scr[rows, Z_F], z_scr[rows, Z_V], z_scr[rows, Z_G],
                         lb, hnorm, st_c)
        mix_scr[rows, POOL_DIM + LRU_DIM:MIX_DIM] = yc.astype(BF16)

    z_scr[:, :Z_POOL.start] = _dot(hn, win_ref[:, :Z_POOL.start])
    hgrn(0)
    z_scr[:, Z_POOL.start:] = _dot(hn, win_ref[:, Z_POOL.start:])

    ya, new_pool = _pool_branch(z_scr[:, Z_POOL], pool_c[...], pos, wp_ref[...], ps_ref[...])
    pool_c[...] = new_pool
    mix_scr[:, 0:POOL_DIM] = ya.astype(BF16)

    yb, new_conv, new_h = _lru_branch(
        z_scr[:, Z_LX], z_scr[:, Z_LG], conv_c[...], h_c[SUBLANES - 1:SUBLANES],
        pos, cw_ref[...], cb_ref[...], wl_ref[...], bl_ref[...], lam_ref[...])
    conv_c[...] = new_conv
    h_c[...] = new_h
    mix_scr[:, POOL_DIM:POOL_DIM + LRU_DIM] = yb.astype(BF16)

    for r0 in range(chunk, tt, chunk):
        hgrn(r0)
    y_ref[...] = x + _dot(mix_scr[...], wout_ref[...])

    @pl.when(i == pl.num_programs(1) - 1)
    def _():
        pool_o_ref[...] = pool_c[...]
        conv_o_ref[...] = conv_c[...]
        h_o_ref[...] = h_c[...]
        for h in range(HG_HEADS):
            s_o_ref[h] = st_c[h].T


class _StateIO:
    def __init__(self, src, dst, seq):
        self.src, self.dst, self.seq = src, dst, seq

    def __getitem__(self, h):
        return self.src[self.seq, h].T

    def __setitem__(self, h, val):
        self.dst[self.seq, h] = val.T


def _mixer_group_kernel(x_ref, ng_ref, win_ref, wp_ref, ps_ref, cw_ref, cb_ref, wl_ref, bl_ref, lam_ref,
                        lbl_ref, hn_ref, wout_ref, pool0_ref, conv0_ref, h0_ref, s0_ref,
                        y_ref, pool_o_ref, conv_o_ref, h_o_ref, s_o_ref, z_scr, mix_scr, *, layer, pos0):
    g, t, d = x_ref.shape
    x = x_ref[...].reshape(g * t, d)
    hn = _rms(x, ng_ref[...]).astype(BF16)
    z_scr[...] = _dot(hn, win_ref[...])
    pos = pos0 + lax.broadcasted_iota(jnp.int32, (t, 1), 0)
    lb = _lower_bound(lbl_ref, layer)
    hnorm = hn_ref[...]
    for s in range(g):
        rows = slice(s * t, (s + 1) * t)
        ya, new_pool = _pool_branch(z_scr[rows, Z_POOL], pool0_ref[s], pos, wp_ref[...], ps_ref[...])
        pool_o_ref[s] = new_pool
        mix_scr[rows, 0:POOL_DIM] = ya.astype(BF16)
        yb, new_conv, new_h = _lru_branch(
            z_scr[rows, Z_LX], z_scr[rows, Z_LG], conv0_ref[s], h0_ref[s, SUBLANES - 1:SUBLANES],
            pos, cw_ref[...], cb_ref[...], wl_ref[...], bl_ref[...], lam_ref[...])
        conv_o_ref[s] = new_conv
        h_o_ref[s] = new_h
        mix_scr[rows, POOL_DIM:POOL_DIM + LRU_DIM] = yb.astype(BF16)
        yc = _hgrn_chunk(z_scr[rows, Z_Q], z_scr[rows, Z_F], z_scr[rows, Z_V], z_scr[rows, Z_G],
                         lb, hnorm, _StateIO(s0_ref, s_o_ref, s))
        mix_scr[rows, POOL_DIM + LRU_DIM:MIX_DIM] = yc.astype(BF16)
    y_ref[...] = (x + _dot(mix_scr[...], wout_ref[...])).reshape(g, t, d)


def _block_diag(w):
    g, c, d = w.shape
    eye = jnp.eye(g, dtype=w.dtype)
    return (eye[:, None, :, None] * w[:, :, None, :]).reshape(g * c, g * d)


STATE_SHAPES = ((POOL_ROWS, POOL_DIM), (CONV_ROWS, LRU_DIM), (SUBLANES, LRU_DIM), (HG_HEADS, HG_DIM, HG_DIM))


def _mixer(x, pos0, pool0, conv0, h0, s0, layer, p):
    bsz, t, d = x.shape
    tt = min(MIX_ROWS, t)
    assert t % tt == 0 and tt % POOL_ROWS == 0
    chunk = min(HG_CHUNK, tt)
    assert tt % chunk == 0
    in_dim = p['w_in'].shape[2]
    depth = p['lb_logits'].shape[0]

    group = min(bsz, MIX_ROWS // t) if t <= HG_CHUNK else 1
    states = [jnp.pad(pool0, ((0, 0), (POOL_ROWS - pool0.shape[1], 0), (0, 0))),
              jnp.pad(conv0, ((0, 0), (CONV_ROWS - conv0.shape[1], 0), (0, 0))),
              jnp.pad(h0[:, None, :], ((0, 0), (SUBLANES - 1, 0), (0, 0))), s0]
    if group > 1:
        assert bsz % group == 0
        kern = functools.partial(_mixer_group_kernel, layer=layer, pos0=pos0)
        state_specs = [pl.BlockSpec((group,) + sh, lambda b, i, sh=sh: (b,) + (0,) * len(sh))
                       for sh in STATE_SHAPES]
        tile = pl.BlockSpec((group, t, d), lambda b, i: (b, 0, 0))
        grid = (bsz // group, 1)
        scratch = [pltpu.VMEM((group * t, in_dim), F32), pltpu.VMEM((group * t, MIX_DIM), BF16)]
    else:
        kern = functools.partial(_mixer_kernel, layer=layer, pos0=pos0, chunk=chunk)
        state_specs = [pl.BlockSpec((None,) + sh, lambda b, i, sh=sh: (b,) + (0,) * len(sh))
                       for sh in STATE_SHAPES]
        tile = pl.BlockSpec((None, tt, d), lambda b, i: (b, i, 0))
        grid = (bsz, t // tt)
        scratch = ([pltpu.VMEM((tt, in_dim), F32), pltpu.VMEM((tt, MIX_DIM), BF16)]
                   + [pltpu.VMEM(sh, F32) for sh in STATE_SHAPES])
    in_specs = [tile, _const_spec((1, d)), _layer_spec((d, in_dim), layer),
                _const_spec((POOL_DIM, POOL_DIM)), _const_spec((1, POOL_DIM)),
                _const_spec((CONV_WIDTH, LRU_DIM)), _const_spec((1, LRU_DIM)),
                _const_spec((LRU_DIM, 2 * LRU_DIM)), _const_spec((1, 2 * LRU_DIM)), _const_spec((1, LRU_DIM)),
                _const_spec((depth, HG_WIDTH)), _const_spec((1, HG_WIDTH)),
                _layer_spec((MIX_DIM, d), layer)] + state_specs
    y, pool_n, conv_n, h_n, s_n = pl.pallas_call(
        kern,
        grid=grid,
        in_specs=in_specs,
        out_specs=[tile] + state_specs,
        out_shape=[jax.ShapeDtypeStruct((bsz, t, d), F32)]
                  + [jax.ShapeDtypeStruct((bsz,) + sh, F32) for sh in STATE_SHAPES],
        scratch_shapes=scratch,
        compiler_params=_params(("parallel", "arbitrary")),
        name="mixer",
    )(x, p['mix_norm'], p['w_in'], p['pool_wbd'], p['pool_scale'], p['conv_w'], p['conv_b'],
      p['lru_wbd'], p['lru_b'], p['lru_lambda'], p['lb_logits'], p['hgrn_norm'], p['w_out'], *states)
    return (y, pool_n[:, POOL_ROWS - POOL_HIST:], conv_n[:, CONV_ROWS - (CONV_WIDTH - 1):],
            h_n[:, SUBLANES - 1], s_n)


def _run(groups, layers, final_norm):
    depth = len(layers)
    xs = [g[0] for g in groups]
    shapes = [x.shape for x in xs]
    d = shapes[0][2]
    new_states = [([], [], [], []) for _ in groups]
    for l, p in enumerate(layers):
        flat = _ffn(xs[0].reshape(-1, d), xs[1].reshape(-1, d), p['ffn1_norm'], p['ffn1_w_gate'],
                    p['ffn1_w_up'], p['ffn1_w_down'], l)
        for k, (_, pos0, pool, conv, lru, hg) in enumerate(groups):
            xs[k], *st = _mixer(flat[k].reshape(shapes[k]), pos0, pool[l], conv[l], lru[l], hg[l], l, p)
            for acc, val in zip(new_states[k], st):
                acc.append(val)
        flat = _ffn(xs[0].reshape(-1, d), xs[1].reshape(-1, d), p['ffn2_norm'], p['ffn2_w_gate'],
                    p['ffn2_w_up'], p['ffn2_w_down'], l, final_norm[None] if l == depth - 1 else None)
        xs = [f.reshape(sh) for f, sh in zip(flat, shapes)]
    return [(x,) + tuple(jnp.stack(a) for a in st) for x, st in zip(xs, new_states)]


def kernel(x_prompt, x_sample, state_pool, state_conv, state_lru, state_hgrn, ffn1_norm, ffn1_w_gate, ffn1_w_up, ffn1_w_down, mix_norm, w_in, pool_w, pool_scale, conv_w, conv_b, lru_w_a, lru_b_a, lru_w_x, lru_b_x, lru_lambda, hgrn_lb_logits, hgrn_norm, w_out, ffn2_norm, ffn2_w_gate, ffn2_w_up, ffn2_w_down, final_norm):
    depth = w_in.shape[0]
    big = {'ffn1_w_gate': _to_bf16(ffn1_w_gate), 'ffn1_w_up': _to_bf16(ffn1_w_up),
           'ffn1_w_down': _to_bf16(ffn1_w_down), 'w_in': _to_bf16(w_in, N_FRONT), 'w_out': _to_bf16(w_out),
           'ffn2_w_gate': _to_bf16(ffn2_w_gate), 'ffn2_w_up': _to_bf16(ffn2_w_up),
           'ffn2_w_down': _to_bf16(ffn2_w_down)}
    layers = []
    for l in range(depth):
        layers.append({
            **big, 'ffn1_norm': ffn1_norm[l][None], 'mix_norm': mix_norm[l][None],
            'pool_wbd': _block_diag(pool_w[l]).astype(BF16), 'pool_scale': pool_scale[l][None],
            'conv_w': conv_w[l], 'conv_b': conv_b[l][None],
            'lru_wbd': jnp.concatenate([_block_diag(lru_w_a[l]), _block_diag(lru_w_x[l])],
                                       axis=1).astype(BF16),
            'lru_b': jnp.concatenate([lru_b_a[l], lru_b_x[l]])[None],
            'lru_lambda': lru_lambda[l][None], 'lb_logits': hgrn_lb_logits,
            'hgrn_norm': hgrn_norm[l][None], 'ffn2_norm': ffn2_norm[l][None],
        })
    bp = x_prompt.shape[0]
    dt = x_prompt.dtype
    zero_pool = jnp.zeros((depth, bp) + state_pool.shape[2:], dt)
    zero_conv = jnp.zeros((depth, bp) + state_conv.shape[2:], dt)
    zero_lru = jnp.zeros((depth, bp) + state_lru.shape[2:], dt)
    zero_hgrn = jnp.zeros((depth, bp) + state_hgrn.shape[2:], dt)
    (y_p, pool_p, conv_p, lru_p, hgrn_p), (y_s, pool_s, conv_s, lru_s, hgrn_s) = _run(
        [(x_prompt, 0, zero_pool, zero_conv, zero_lru, zero_hgrn),
         (x_sample, PAST_LEN, state_pool, state_conv, state_lru, state_hgrn)], layers, final_norm)
    return (y_p, y_s, pool_p, conv_p, lru_p, hgrn_p, pool_s, conv_s, lru_s, hgrn_s)
```
